```python
import jax, jax.numpy as jnp
from jax import lax
import numpy as np

D_MODEL = 2048
BATCH = 8
SEQ = 4096
DEPTH = 4

N_MIXERS = 2
N_HEADS = 16
HEAD_DIM = D_MODEL // N_HEADS
D_FF = 5632
BLOCK_Q = 128
EPS = 1e-6
FFN_RES = 0.5

kernel_name = "fox_stickbreak_macaron_hybrid"


def rms_norm(x, g):
    xf = x.astype(jnp.float32)
    y = xf * lax.rsqrt(jnp.mean(xf * xf, axis=-1, keepdims=True) + EPS)
    return (y * g.astype(jnp.float32)).astype(x.dtype)


def swiglu(h, w_in, w_out):
    gate, up = jnp.split(h @ w_in, 2, axis=-1)
    return (jax.nn.silu(gate) * up) @ w_out


def split_heads(t):
    b, s, _ = t.shape
    return t.reshape(b, s, N_HEADS, HEAD_DIM).transpose(0, 2, 1, 3)


def merge_heads(t):
    b, h, s, d = t.shape
    return t.transpose(0, 2, 1, 3).reshape(b, s, h * d)


def fox_attention(h, w_in, b_f, qk_g):
    seq = h.shape[1]
    proj = h @ w_in
    q, k, v, f_logit = jnp.split(proj, [D_MODEL, 2 * D_MODEL, 3 * D_MODEL], axis=-1)
    q = rms_norm(split_heads(q), qk_g[0])
    k = rms_norm(split_heads(k), qk_g[1])
    v = split_heads(v)
    log_f = jax.nn.log_sigmoid((f_logit + b_f).astype(jnp.float32))
    c = jnp.cumsum(log_f, axis=1).transpose(0, 2, 1)
    scale = HEAD_DIM ** -0.5
    outs = []
    for blk in range(seq // BLOCK_Q):
        start, end = blk * BLOCK_Q, (blk + 1) * BLOCK_Q
        s = jnp.einsum('bhqd,bhkd->bhqk', q[:, :, start:end], k[:, :, :end],
                       preferred_element_type=jnp.float32) * scale
        s = s + c[:, :, start:end, None] - c[:, :, None, :end]
        mask = (start + jnp.arange(BLOCK_Q))[:, None] >= jnp.arange(end)[None, :]
        p = jax.nn.softmax(jnp.where(mask, s, -jnp.inf), axis=-1)
        outs.append(jnp.einsum('bhqk,bhkd->bhqd', p.astype(v.dtype), v[:, :, :end]))
    return merge_heads(jnp.concatenate(outs, axis=2))


def stick_breaking_attention(h, w_in):
    seq = h.shape[1]
    q, k, v = jnp.split(h @ w_in, 3, axis=-1)
    q, k, v = split_heads(q), split_heads(k), split_heads(v)
    scale = HEAD_DIM ** -0.5
    outs = []
    for blk in range(seq // BLOCK_Q):
        start, end = blk * BLOCK_Q, (blk + 1) * BLOCK_Q
        z = jnp.einsum('bhqd,bhkd->bhqk', q[:, :, start:end], k[:, :, :end],
                       preferred_element_type=jnp.float32) * scale
        mask = jnp.arange(end)[None, :] < (start + jnp.arange(BLOCK_Q))[:, None]
        log_1m_beta = jnp.where(mask, jax.nn.log_sigmoid(-z), 0.0)
        tail = lax.cumsum(log_1m_beta, axis=3, reverse=True) - log_1m_beta
        log_a = jnp.where(mask, jax.nn.log_sigmoid(z) + tail, -jnp.inf)
        a = jnp.exp(log_a)
        outs.append(jnp.einsum('bhqk,bhkd->bhqd', a.astype(v.dtype), v[:, :, :end]))
    return merge_heads(jnp.concatenate(outs, axis=2))


def setup_inputs(seed: int = 0) -> dict:
    key = jax.random.key(seed)
    ks = jax.random.split(key, 10)
    n_fox = len(range(0, DEPTH, N_MIXERS))
    n_sb = DEPTH - n_fox
    f32 = jnp.float32
    x = jax.random.normal(ks[0], (BATCH, SEQ, D_MODEL), f32)
    norm_g = 1.0 + 0.02 * jax.random.normal(ks[1], (DEPTH, 3, D_MODEL), f32)
    ffn_w_in = jax.random.normal(ks[2], (DEPTH, 2, D_MODEL, 2 * D_FF), f32) * D_MODEL ** -0.5
    ffn_w_out = jax.random.normal(ks[3], (DEPTH, 2, D_FF, D_MODEL), f32) * D_FF ** -0.5
    fox_w_in = jax.random.normal(ks[4], (n_fox, D_MODEL, 3 * D_MODEL + N_HEADS), f32) * D_MODEL ** -0.5
    fox_b_f = 2.0 + jax.random.normal(ks[5], (n_fox, N_HEADS), f32)
    fox_qk_g = 1.0 + 0.02 * jax.random.normal(ks[6], (n_fox, 2, HEAD_DIM), f32)
    sb_w_in = jax.random.normal(ks[7], (n_sb, D_MODEL, 3 * D_MODEL), f32) * D_MODEL ** -0.5
    w_o = jax.random.normal(ks[8], (DEPTH, D_MODEL, D_MODEL), f32) * D_MODEL ** -0.5
    return {"x": x, "norm_g": norm_g, "ffn_w_in": ffn_w_in, "ffn_w_out": ffn_w_out,
            "fox_w_in": fox_w_in, "fox_b_f": fox_b_f, "fox_qk_g": fox_qk_g,
            "sb_w_in": sb_w_in, "w_o": w_o}


def reference(x, norm_g, ffn_w_in, ffn_w_out, fox_w_in, fox_b_f, fox_qk_g, sb_w_in, w_o):
    for i in range(DEPTH):
        g = norm_g[i]
        x = x + FFN_RES * swiglu(rms_norm(x, g[0]), ffn_w_in[i, 0], ffn_w_out[i, 0])
        h = rms_norm(x, g[1])
        j = i // N_MIXERS
        if i % N_MIXERS == 0:
            m = fox_attention(h, fox_w_in[j], fox_b_f[j], fox_qk_g[j])
        else:
            m = stick_breaking_attention(h, sb_w_in[j])
        x = x + m @ w_o[i]
        x = x + FFN_RES * swiglu(rms_norm(x, g[2]), ffn_w_in[i, 1], ffn_w_out[i, 1])
    return x
```

```python
import functools

import jax
import jax.numpy as jnp
from jax import lax
from jax.experimental import pallas as pl
from jax.experimental.pallas import tpu as pltpu

N_HEADS = 16
N_MIXERS = 2
EPS = 1e-6
FFN_RES = 0.5
LANES = 128
VMEM_LIMIT = 56 * 1024 * 1024

F32 = jnp.float32
BF16 = jnp.bfloat16


def _rms_norm_rows(x, g):
    ms = jnp.mean(x * x, axis=-1, keepdims=True)
    return x * lax.rsqrt(ms + EPS) * g


def _params(sem):
    return pltpu.CompilerParams(dimension_semantics=sem, vmem_limit_bytes=VMEM_LIMIT)


def _ffn_kernel(x_ref, g_ref, wg_ref, wu_ref, wo_ref, o_ref, h_ref):
    @pl.when(pl.program_id(1) == 0)
    def _():
        x = x_ref[...]
        h_ref[...] = _rms_norm_rows(x, g_ref[...]).astype(BF16)
        o_ref[...] = x

    h = h_ref[...]
    gate = jnp.dot(h, wg_ref[...], preferred_element_type=F32)
    up = jnp.dot(h, wu_ref[...], preferred_element_type=F32)
    act = ((gate * jax.nn.sigmoid(gate)) * (up * FFN_RES)).astype(BF16)
    o_ref[...] += jnp.dot(act, wo_ref[...], preferred_element_type=F32)


def _ffn(x2d, g, w_in, w_out, *, tm=512, tf=512):
    t, d = x2d.shape
    d_ff = w_out.shape[0]
    nf = d_ff // tf
    return pl.pallas_call(
        _ffn_kernel,
        grid=(t // tm, nf),
        in_specs=[
            pl.BlockSpec((tm, d), lambda i, f: (i, 0)),
            pl.BlockSpec((1, d), lambda i, f: (0, 0)),
            pl.BlockSpec((d, tf), lambda i, f: (0, f)),
            pl.BlockSpec((d, tf), lambda i, f: (0, nf + f)),
            pl.BlockSpec((tf, d), lambda i, f: (f, 0)),
        ],
        out_specs=pl.BlockSpec((tm, d), lambda i, f: (i, 0)),
        out_shape=jax.ShapeDtypeStruct((t, d), F32),
        scratch_shapes=[pltpu.VMEM((tm, d), BF16)],
        compiler_params=_params(("parallel", "arbitrary")),
        name="ffn",
    )(x2d, g.reshape(1, d), w_in, w_in, w_out)


def _proj_kernel(x_ref, g_ref, wq_ref, wk_ref, wv_ref, *rest, fox, head_dim):
    if fox:
        wf_ref, bf_ref, gq_ref, gk_ref, q_ref, k_ref, v_ref, lf_ref, h_ref = rest
    else:
        q_ref, k_ref, v_ref, h_ref = rest

    @pl.when(pl.program_id(1) == 0)
    def _():
        h = _rms_norm_rows(x_ref[...], g_ref[...]).astype(BF16)
        h_ref[...] = h
        if fox:
            f_logit = jnp.dot(h, wf_ref[...], preferred_element_type=F32) + bf_ref[...]
            lf_ref[...] = jax.nn.log_sigmoid(f_logit)

    h = h_ref[...]
    q = jnp.dot(h, wq_ref[...], preferred_element_type=F32)
    k = jnp.dot(h, wk_ref[...], preferred_element_type=F32)
    v = jnp.dot(h, wv_ref[...], preferred_element_type=F32)
    v_ref[...] = v.astype(BF16)
    if fox:
        for hh in range(q.shape[1] // head_dim):
            sl = slice(hh * head_dim, (hh + 1) * head_dim)
            q_ref[:, sl] = _rms_norm_rows(q[:, sl], gq_ref[...]).astype(BF16)
            k_ref[:, sl] = _rms_norm_rows(k[:, sl], gk_ref[...]).astype(BF16)
    else:
        q_ref[...] = q.astype(BF16)
        k_ref[...] = k.astype(BF16)


def _proj(x2d, g, w_qkv, fox_extra=None, *, tm=512, tn=512):
    t, d = x2d.shape
    head_dim = d // N_HEADS
    nn = d // tn
    fox = fox_extra is not None
    in_specs = [
        pl.BlockSpec((tm, d), lambda i, n: (i, 0)),
        pl.BlockSpec((1, d), lambda i, n: (0, 0)),
        pl.BlockSpec((d, tn), lambda i, n: (0, n)),
        pl.BlockSpec((d, tn), lambda i, n: (0, nn + n)),
        pl.BlockSpec((d, tn), lambda i, n: (0, 2 * nn + n)),
    ]
    args = [x2d, g.reshape(1, d), w_qkv, w_qkv, w_qkv]
    out_specs = [pl.BlockSpec((tm, tn), lambda i, n: (i, n))] * 3
    out_shape = [jax.ShapeDtypeStruct((t, d), BF16)] * 3
    if fox:
        w_f, b_f, g_q, g_k = fox_extra
        in_specs += [
            pl.BlockSpec((d, LANES), lambda i, n: (0, 0)),
            pl.BlockSpec((1, LANES), lambda i, n: (0, 0)),
            pl.BlockSpec((1, head_dim), lambda i, n: (0, 0)),
            pl.BlockSpec((1, head_dim), lambda i, n: (0, 0)),
        ]
        args += [w_f, b_f, g_q, g_k]
        out_specs = out_specs + [pl.BlockSpec((tm, LANES), lambda i, n: (i, 0))]
        out_shape = out_shape + [jax.ShapeDtypeStruct((t, LANES), F32)]
    return pl.pallas_call(
        functools.partial(_proj_kernel, fox=fox, head_dim=head_dim),
        grid=(t // tm, nn),
        in_specs=in_specs,
        out_specs=out_specs,
        out_shape=out_shape,
        scratch_shapes=[pltpu.VMEM((tm, d), BF16)],
        compiler_params=_params(("parallel", "arbitrary")),
        name="proj_fox" if fox else "proj_sb",
    )(*args)


def _split3(x):
    hi = x.astype(BF16)
    r = x - hi.astype(F32)
    mid = r.astype(BF16)
    lo = (r - mid.astype(F32)).astype(BF16)
    return hi, mid, lo


def _cumsum_kernel(lf_ref, c_ref, carry_ref):
    @pl.when(pl.program_id(1) == 0)
    def _():
        carry_ref[...] = jnp.zeros_like(carry_ref)

    tc = lf_ref.shape[1]
    row = lax.broadcasted_iota(jnp.int32, (tc, tc), 0)
    col = lax.broadcasted_iota(jnp.int32, (tc, tc), 1)
    tri = (col <= row).astype(BF16)
    hi, mid, lo = _split3(lf_ref[0])
    c = (jnp.dot(tri, lo, preferred_element_type=F32)
         + jnp.dot(tri, mid, preferred_element_type=F32)
         + jnp.dot(tri, hi, preferred_element_type=F32)) + carry_ref[...]
    c_ref[0] = c
    carry_ref[...] = c[tc - 1:tc, :]


def _cumsum_seq(lf, *, tc=512):
    b, s, w = lf.shape
    return pl.pallas_call(
        _cumsum_kernel,
        grid=(b, s // tc),
        in_specs=[pl.BlockSpec((1, tc, w), lambda i, j: (i, j, 0))],
        out_specs=pl.BlockSpec((1, tc, w), lambda i, j: (i, j, 0)),
        out_shape=jax.ShapeDtypeStruct((b, s, w), F32),
        scratch_shapes=[pltpu.VMEM((1, w), F32)],
        compiler_params=_params(("parallel", "arbitrary")),
        name="cumsum",
    )(lf)


def _lane_tile(x, n):
    return x if n == LANES else jnp.concatenate([x] * (n // LANES), axis=1)


def _causal_mask(tq, tk, strict):
    row = lax.broadcasted_iota(jnp.int32, (tq, tk), 0)
    col = lax.broadcasted_iota(jnp.int32, (tq, tk), 1)
    return (col < row) if strict else (col <= row)


def _qk(q, k):
    return lax.dot_general(q, k, (((1,), (1,)), ((), ())), preferred_element_type=F32)


def _fox_attn_kernel(q_ref, k_ref, v_ref, cq_ref, ck_ref, o_ref, m_ref, l_ref, acc_ref,
                     *, scale, tk):
    i = pl.program_id(2)
    tq = q_ref.shape[1]
    q = q_ref[0]
    c_col = jnp.concatenate(
        [jnp.broadcast_to(cq_ref[0, 0, pl.ds(i * (tq // LANES) + r, 1), :], (LANES, LANES)).T
         for r in range(tq // LANES)], axis=0)
    c_col = _lane_tile(c_col, tk)

    def step(j, masked):
        k = k_ref[0, pl.ds(pl.multiple_of(j * tk, tk), tk), :]
        v = v_ref[0, pl.ds(pl.multiple_of(j * tk, tk), tk), :]
        s = _qk(q, k) * scale + c_col - ck_ref[0, 0, pl.ds(j, 1), :]
        if masked:
            s = jnp.where(_causal_mask(tq, tk, strict=False), s, -jnp.inf)
        m_prev = m_ref[...]
        m_new = jnp.maximum(m_prev, jnp.max(s, axis=1, keepdims=True))
        p = jnp.exp(s - _lane_tile(m_new, tk))
        alpha = jnp.exp(m_prev - m_new)
        l_ref[...] = alpha * l_ref[...] + jnp.sum(p, axis=1, keepdims=True)
        acc_ref[...] = alpha * acc_ref[...] + jnp.dot(p.astype(BF16), v,
                                                      preferred_element_type=F32)
        m_ref[...] = m_new

    m_ref[...] = jnp.full_like(m_ref, -jnp.inf)
    l_ref[...] = jnp.zeros_like(l_ref)
    acc_ref[...] = jnp.zeros_like(acc_ref)
    step(i, True)

    def body(jj, carry):
        step(i - 1 - jj, False)
        return carry

    lax.fori_loop(0, i, body, 0)
    o_ref[0] = (acc_ref[...] / l_ref[...]).astype(o_ref.dtype)


def _sb_attn_kernel(q_ref, k_ref, v_ref, o_ref, r_ref, acc_ref, *, scale, tk):
    i = pl.program_id(2)
    tq = q_ref.shape[1]
    q = q_ref[0]
    row = lax.broadcasted_iota(jnp.int32, (2 * tk, tk), 0)
    col = lax.broadcasted_iota(jnp.int32, (2 * tk, tk), 1)
    u2 = ((row > col) & (row < tk) | (row - tk > col)).astype(BF16)

    def step(j, masked):
        k = k_ref[0, pl.ds(pl.multiple_of(j * tk, tk), tk), :]
        v = v_ref[0, pl.ds(pl.multiple_of(j * tk, tk), tk), :]
        z = _qk(q, k) * scale
        l1m = -(jnp.maximum(z, 0.0) + jnp.log(1.0 + jnp.exp(-jnp.abs(z))))
        if masked:
            mask = _causal_mask(tq, tk, strict=True)
            l1m = jnp.where(mask, l1m, 0.0)
        hi = l1m.astype(BF16)
        lo = (l1m - hi.astype(F32)).astype(BF16)
        tail = jnp.dot(jnp.concatenate([hi, lo], axis=1), u2, preferred_element_type=F32)
        r_prev = r_ref[...]
        log_a = z + l1m + tail + _lane_tile(r_prev, tk)
        a = jnp.exp(log_a)
        if masked:
            a = jnp.where(mask, a, 0.0)
        acc_ref[...] += jnp.dot(a.astype(BF16), v, preferred_element_type=F32)
        r_ref[...] = r_prev + jnp.sum(l1m, axis=1, keepdims=True)

    r_ref[...] = jnp.zeros_like(r_ref)
    acc_ref[...] = jnp.zeros_like(acc_ref)
    step(i, True)

    def body(jj, carry):
        step(i - 1 - jj, False)
        return carry

    lax.fori_loop(0, i, body, 0)
    o_ref[0] = acc_ref[...].astype(o_ref.dtype)


def _attention(q, k, v, c_rows=None, *, tq=256, tk=256):
    b, s, d = q.shape
    head_dim = d // N_HEADS
    scale = head_dim ** -0.5
    qo_spec = pl.BlockSpec((1, tq, head_dim), lambda bi, h, i: (bi, i, h))
    kv_spec = pl.BlockSpec((1, s, head_dim), lambda bi, h, i: (bi, 0, h))
    in_specs = [qo_spec, kv_spec, kv_spec]
    args = [q, k, v]
    acc = pltpu.VMEM((tq, head_dim), F32)
    stat = pltpu.VMEM((tq, LANES), F32)
    if c_rows is not None:
        in_specs += [
            pl.BlockSpec((1, 1, s // LANES, LANES), lambda bi, h, i: (bi, h, 0, 0)),
            pl.BlockSpec((1, 1, s // tk, tk), lambda bi, h, i: (bi, h, 0, 0)),
        ]
        args += [c_rows.reshape(b, N_HEADS, s // LANES, LANES),
                 c_rows.reshape(b, N_HEADS, s // tk, tk)]
        kern = functools.partial(_fox_attn_kernel, scale=scale, tk=tk)
        scratch = [stat, stat, acc]
        name = "fox_attn"
    else:
        kern = functools.partial(_sb_attn_kernel, scale=scale, tk=tk)
        scratch = [stat, acc]
        name = "sb_attn"
    return pl.pallas_call(
        kern,
        grid=(b, N_HEADS, s // tq),
        in_specs=in_specs,
        out_specs=qo_spec,
        out_shape=jax.ShapeDtypeStruct((b, s, d), BF16),
        scratch_shapes=scratch,
        compiler_params=_params(("parallel", "parallel", "arbitrary")),
        name=name,
    )(*args)


def _oproj_kernel(m_ref, w_ref, x_ref, o_ref):
    o_ref[...] = x_ref[...] + jnp.dot(m_ref[...], w_ref[...], preferred_element_type=F32)


def _oproj(m2d, w, x2d, *, tm=512, tn=512):
    t, d = x2d.shape
    return pl.pallas_call(
        _oproj_kernel,
        grid=(t // tm, d // tn),
        in_specs=[
            pl.BlockSpec((tm, d), lambda i, n: (i, 0)),
            pl.BlockSpec((d, tn), lambda i, n: (0, n)),
            pl.BlockSpec((tm, tn), lambda i, n: (i, n)),
        ],
        out_specs=pl.BlockSpec((tm, tn), lambda i, n: (i, n)),
        out_shape=jax.ShapeDtypeStruct((t, d), F32),
        compiler_params=_params(("parallel", "arbitrary")),
        name="oproj",
    )(m2d, w, x2d)


def kernel(x, norm_g, ffn_w_in, ffn_w_out, fox_w_in, fox_b_f, fox_qk_g, sb_w_in, w_o):
    b, s, d = x.shape
    depth = norm_g.shape[0]
    t = b * s
    x2d = x.reshape(t, d)
    for i in range(depth):
        g = norm_g[i]
        x2d = _ffn(x2d, g[0], ffn_w_in[i, 0].astype(BF16), ffn_w_out[i, 0].astype(BF16))
        j = i // N_MIXERS
        if i % N_MIXERS == 0:
            w = fox_w_in[j]
            w_f = jnp.pad(w[:, 3 * d:], ((0, 0), (0, LANES - N_HEADS))).astype(BF16)
            b_f = jnp.pad(fox_b_f[j], (0, LANES - N_HEADS)).reshape(1, LANES)
            q, k, v, lf = _proj(x2d, g[1], w[:, :3 * d].astype(BF16),
                                (w_f, b_f, fox_qk_g[j, 0:1], fox_qk_g[j, 1:2]))
            c = _cumsum_seq(lf.reshape(b, s, LANES))
            c_rows = jnp.transpose(c[:, :, :N_HEADS], (0, 2, 1))
            m = _attention(q.reshape(b, s, d), k.reshape(b, s, d), v.reshape(b, s, d), c_rows)
        else:
            q, k, v = _proj(x2d, g[1], sb_w_in[j].astype(BF16))
            m = _attention(q.reshape(b, s, d), k.reshape(b, s, d), v.reshape(b, s, d))
        x2d = _oproj(m.reshape(t, d), w_o[i].astype(BF16), x2d)
        x2d = _ffn(x2d, g[2], ffn_w_in[i, 1].astype(BF16), ffn_w_out[i, 1].astype(BF16))
    return x2d.reshape(b, s, d)
```

```python
import functools

import jax
import jax.numpy as jnp
from jax import lax
from jax.experimental import pallas as pl
from jax.experimental.pallas import tpu as pltpu

N_HEADS = 16
N_MIXERS = 2
EPS = 1e-6
FFN_RES = 0.5
LANES = 128
VMEM_LIMIT = 56 * 1024 * 1024
EXP_ZERO = -104.0
BF16_ULP = 2.0 ** -8

F32 = jnp.float32
BF16 = jnp.bfloat16


def _rms_norm_rows(x, g):
    ms = jnp.mean(x * x, axis=-1, keepdims=True)
    return x * lax.rsqrt(ms + EPS) * g


def _params(sem):
    return pltpu.CompilerParams(dimension_semantics=sem, vmem_limit_bytes=VMEM_LIMIT)


def _ffn_kernel(x_ref, g_ref, wg_ref, wu_ref, wo_ref, o_ref, h_ref):
    @pl.when(pl.program_id(1) == 0)
    def _():
        x = x_ref[...]
        h_ref[...] = _rms_norm_rows(x, g_ref[...]).astype(BF16)
        o_ref[...] = x

    h = h_ref[...]
    gate = jnp.dot(h, wg_ref[...], preferred_element_type=F32)
    up = jnp.dot(h, wu_ref[...], preferred_element_type=F32)
    act = ((gate * jax.nn.sigmoid(gate)) * (up * FFN_RES)).astype(BF16)
    o_ref[...] += jnp.dot(act, wo_ref[...], preferred_element_type=F32)


def _ffn(x2d, g, w_in, w_out, *, tm=1024, tf=512):
    t, d = x2d.shape
    d_ff = w_out.shape[0]
    nf = d_ff // tf
    w_in = w_in.astype(BF16)
    return pl.pallas_call(
        _ffn_kernel,
        grid=(t // tm, nf),
        in_specs=[
            pl.BlockSpec((tm, d), lambda i, f: (i, 0)),
            pl.BlockSpec((1, d), lambda i, f: (0, 0)),
            pl.BlockSpec((d, tf), lambda i, f: (0, f)),
            pl.BlockSpec((d, tf), lambda i, f: (0, nf + f)),
            pl.BlockSpec((tf, d), lambda i, f: (f, 0)),
        ],
        out_specs=pl.BlockSpec((tm, d), lambda i, f: (i, 0)),
        out_shape=jax.ShapeDtypeStruct((t, d), F32),
        scratch_shapes=[pltpu.VMEM((tm, d), BF16)],
        compiler_params=_params(("parallel", "arbitrary")),
        name="ffn",
    )(x2d, g.reshape(1, d), w_in, w_in, w_out.astype(BF16))


def _proj_kernel(x_ref, g_ref, wq_ref, wk_ref, wv_ref, *rest, fox, head_dim):
    if fox:
        wf_ref, bf_ref, gq_ref, gk_ref, q_ref, k_ref, v_ref, lf_ref, h_ref = rest
    else:
        q_ref, k_ref, v_ref, h_ref = rest
    scale = head_dim ** -0.5

    @pl.when(pl.program_id(1) == 0)
    def _():
        h = _rms_norm_rows(x_ref[...], g_ref[...]).astype(BF16)
        h_ref[...] = h
        if fox:
            f_logit = jnp.dot(h, wf_ref[...], preferred_element_type=F32) + bf_ref[...]
            lf_ref[...] = jax.nn.log_sigmoid(f_logit)

    h = h_ref[...]
    q = jnp.dot(h, wq_ref[...], preferred_element_type=F32)
    k = jnp.dot(h, wk_ref[...], preferred_element_type=F32)
    v = jnp.dot(h, wv_ref[...], preferred_element_type=F32)
    v_ref[...] = v.astype(BF16)
    if fox:
        for hh in range(q.shape[1] // head_dim):
            sl = slice(hh * head_dim, (hh + 1) * head_dim)
            q_ref[:, sl] = (_rms_norm_rows(q[:, sl], gq_ref[...]) * scale).astype(BF16)
            k_ref[:, sl] = _rms_norm_rows(k[:, sl], gk_ref[...]).astype(BF16)
    else:
        q_ref[...] = (q * scale).astype(BF16)
        k_ref[...] = k.astype(BF16)


def _proj(x2d, g, w_qkv, fox_extra=None, *, tm=512, tn=512):
    t, d = x2d.shape
    head_dim = d // N_HEADS
    nn = d // tn
    fox = fox_extra is not None
    in_specs = [
        pl.BlockSpec((tm, d), lambda i, n: (i, 0)),
        pl.BlockSpec((1, d), lambda i, n: (0, 0)),
        pl.BlockSpec((d, tn), lambda i, n: (0, n)),
        pl.BlockSpec((d, tn), lambda i, n: (0, nn + n)),
        pl.BlockSpec((d, tn), lambda i, n: (0, 2 * nn + n)),
    ]
    args = [x2d, g.reshape(1, d), w_qkv, w_qkv, w_qkv]
    out_specs = [pl.BlockSpec((tm, tn), lambda i, n: (i, n))] * 3
    out_shape = [jax.ShapeDtypeStruct((t, d), BF16)] * 3
    if fox:
        w_f, b_f, g_q, g_k = fox_extra
        in_specs += [
            pl.BlockSpec((d, LANES), lambda i, n: (0, 0)),
            pl.BlockSpec((1, LANES), lambda i, n: (0, 0)),
            pl.BlockSpec((1, head_dim), lambda i, n: (0, 0)),
            pl.BlockSpec((1, head_dim), lambda i, n: (0, 0)),
        ]
        args += [w_f, b_f, g_q, g_k]
        out_specs = out_specs + [pl.BlockSpec((tm, LANES), lambda i, n: (i, 0))]
        out_shape = out_shape + [jax.ShapeDtypeStruct((t, LANES), F32)]
    return pl.pallas_call(
        functools.partial(_proj_kernel, fox=fox, head_dim=head_dim),
        grid=(t // tm, nn),
        in_specs=in_specs,
        out_specs=out_specs,
        out_shape=out_shape,
        scratch_shapes=[pltpu.VMEM((tm, d), BF16)],
        compiler_params=_params(("parallel", "arbitrary")),
        name="proj_fox" if fox else "proj_sb",
    )(*args)


N_SPLIT = 3


def _split3(x):
    hi = x.astype(BF16)
    r = x - hi.astype(F32)
    mid = r.astype(BF16)
    lo = (r - mid.astype(F32)).astype(BF16)
    return hi, mid, lo


def _cumsum_kernel(lf_ref, c_ref, aq_ref, ak_ref, carry_ref):
    @pl.when(pl.program_id(1) == 0)
    def _():
        carry_ref[...] = jnp.zeros_like(carry_ref)

    tc = lf_ref.shape[1]
    d = aq_ref.shape[2]
    row = lax.broadcasted_iota(jnp.int32, (tc, tc), 0)
    col = lax.broadcasted_iota(jnp.int32, (tc, tc), 1)
    tri = (col <= row).astype(BF16)
    hi, mid, lo = _split3(lf_ref[0])
    c = (jnp.dot(tri, lo, preferred_element_type=F32)
         + jnp.dot(tri, mid, preferred_element_type=F32)
         + jnp.dot(tri, hi, preferred_element_type=F32)) + carry_ref[...]
    c_ref[0] = c
    carry_ref[...] = c[tc - 1:tc, :]

    head = lax.broadcasted_iota(jnp.int32, (LANES, d), 0)
    lane = lax.broadcasted_iota(jnp.int32, (LANES, d), 1)
    parts = _split3(c)

    def scatter(offset):
        out = None
        for p, part in enumerate(parts):
            sel = (lane == head * LANES + offset + p).astype(BF16)
            y = jnp.dot(part, sel, preferred_element_type=F32)
            out = y if out is None else out + y
        return out

    lane_in_head = lax.broadcasted_iota(jnp.int32, (tc, d), 1) % LANES
    ones_q = (lane_in_head < N_SPLIT).astype(F32)
    ones_k = ((lane_in_head >= N_SPLIT) & (lane_in_head < 2 * N_SPLIT)).astype(F32)
    aq_ref[0] = (scatter(N_SPLIT) + ones_q).astype(BF16)
    ak_ref[0] = (ones_k - scatter(0)).astype(BF16)


def _cumsum_seq(lf, d, *, tc=512):
    b, s, w = lf.shape
    aug_spec = pl.BlockSpec((1, tc, d), lambda i, j: (i, j, 0))
    return pl.pallas_call(
        _cumsum_kernel,
        grid=(b, s // tc),
        in_specs=[pl.BlockSpec((1, tc, w), lambda i, j: (i, j, 0))],
        out_specs=[pl.BlockSpec((1, tc, w), lambda i, j: (i, j, 0)), aug_spec, aug_spec],
        out_shape=[jax.ShapeDtypeStruct((b, s, w), F32),
                   jax.ShapeDtypeStruct((b, s, d), BF16),
                   jax.ShapeDtypeStruct((b, s, d), BF16)],
        scratch_shapes=[pltpu.VMEM((1, w), F32)],
        compiler_params=_params(("parallel", "arbitrary")),
        name="cumsum",
    )(lf)


def _lane_tile(x, n):
    return x if n == LANES else jnp.concatenate([x] * (n // LANES), axis=1)


def _causal_mask(t, strict):
    row = lax.broadcasted_iota(jnp.int32, (t, t), 0)
    col = lax.broadcasted_iota(jnp.int32, (t, t), 1)
    return (col < row) if strict else (col <= row)


def _qk(q, k):
    return lax.dot_general(q, k, (((1,), (1,)), ((), ())), preferred_element_type=F32)


def _head(g):
    return slice(g * LANES, (g + 1) * LANES)


def _fox_attn_kernel(nb_ref, q_ref, aq_ref, k_ref, ak_ref, v_ref, o_ref,
                     m_ref, acc_ref, s_ref, *, heads):
    i = pl.program_id(2)
    t = q_ref.shape[1]
    step_id = (pl.program_id(0) * pl.num_programs(1) + pl.program_id(1)) * pl.num_programs(2) + i
    n_off = nb_ref[step_id]
    ones = jnp.ones((t, LANES), BF16)

    def score(j):
        keys = pl.ds(pl.multiple_of(j * t, t), t)
        for g in range(heads):
            qa = jnp.concatenate([q_ref[0, :, _head(g)], aq_ref[0, :, _head(g)]], axis=1)
            ka = jnp.concatenate([k_ref[0, keys, _head(g)], ak_ref[0, keys, _head(g)]], axis=1)
            s_ref[g] = _qk(qa, ka)

    def step(j, masked):
        s_cur = [s_ref[g] for g in range(heads)]
        score(jnp.maximum(j - 1, 0))
        keys = pl.ds(pl.multiple_of(j * t, t), t)
        for g in range(heads):
            s = s_cur[g]
            if masked:
                s = jnp.where(_causal_mask(t, strict=False), s, -jnp.inf)
            m_prev = m_ref[g]
            m_new = jnp.maximum(m_prev, jnp.max(s, axis=1, keepdims=True))
            p = jnp.exp(s - _lane_tile(m_new, t))
            alpha = jnp.exp(m_prev - m_new)
            va = jnp.concatenate([v_ref[0, keys, _head(g)], ones], axis=1)
            acc_ref[g] = _lane_tile(alpha, 2 * LANES) * acc_ref[g] + jnp.dot(
                p.astype(BF16), va, preferred_element_type=F32)
            m_ref[g] = m_new

    m_ref[...] = jnp.full_like(m_ref, -jnp.inf)
    acc_ref[...] = jnp.zeros_like(acc_ref)
    score(i)
    step(i, True)

    def body(n, carry):
        step(i - 1 - n, False)
        return carry

    lax.fori_loop(0, n_off, body, 0)
    for g in range(heads):
        acc = acc_ref[g]
        o_ref[0, :, _head(g)] = (acc[:, :LANES] / acc[:, LANES:]).astype(o_ref.dtype)


def _sb_attn_kernel(q_ref, k_ref, v_ref, o_ref, r_ref, acc_ref, *, heads):
    i = pl.program_id(2)
    t = q_ref.shape[1]
    row = lax.broadcasted_iota(jnp.int32, (2 * t, t), 0)
    col = lax.broadcasted_iota(jnp.int32, (2 * t, t), 1)
    u2 = (((row >= col) & (row < t)) | (row - t >= col)).astype(BF16)

    def tile(j, masked):
        keys = pl.ds(pl.multiple_of(j * t, t), t)
        mask = _causal_mask(t, strict=True) if masked else None
        zs = [_qk(q_ref[0, :, _head(g)], k_ref[0, keys, _head(g)]) for g in range(heads)]
        parts = []
        for z in zs:
            l1m = -(jnp.maximum(z, 0.0) + jnp.log(1.0 + jnp.exp(-jnp.abs(z))))
            if masked:
                l1m = jnp.where(mask, l1m, 0.0)
            hi = l1m.astype(BF16)
            lo = (l1m - hi.astype(F32)).astype(BF16)
            parts.append(jnp.concatenate([hi, lo], axis=1))
        incls = [jnp.dot(p, u2, preferred_element_type=F32) for p in parts]
        for g in range(heads):
            r_prev = r_ref[g]
            a = jnp.exp(zs[g] + incls[g] + _lane_tile(r_prev, t))
            if masked:
                a = jnp.where(mask, a, 0.0)
            acc_ref[g] += jnp.dot(a.astype(BF16), v_ref[0, keys, _head(g)],
                                  preferred_element_type=F32)
            r_ref[g] = r_prev + incls[g][:, 0:1]

    r_ref[...] = jnp.zeros_like(r_ref)
    acc_ref[...] = jnp.zeros_like(acc_ref)
    tile(i, True)

    def cond(carry):
        j, live = carry
        return jnp.logical_and(j >= 0, live)

    def body(carry):
        j, _ = carry
        tile(j, False)
        return j - 1, jnp.max(r_ref[...]) >= EXP_ZERO

    lax.while_loop(cond, body, (i - 1, jnp.max(r_ref[...]) >= EXP_ZERO))
    for g in range(heads):
        o_ref[0, :, _head(g)] = acc_ref[g].astype(o_ref.dtype)


def _fox_block_counts(c, score_bound, t, heads):
    b, s, _ = c.shape
    n_blk = s // t
    ch = c[:, :, :N_HEADS]
    c_first_q = ch[:, ::t, :]
    c_last_k = ch[:, t - 1::t, :]
    decay = c_first_q[:, :, None, :] - c_last_k[:, None, :, :]
    live = decay >= (EXP_ZERO - 1.0) - 2.0 * score_bound
    qi = jnp.arange(n_blk)[None, :, None, None]
    kj = jnp.arange(n_blk)[None, None, :, None]
    live = jnp.logical_and(live, kj < qi)
    n = jnp.sum(live, axis=2).astype(jnp.int32)
    n = n.reshape(b, n_blk, N_HEADS // heads, heads).max(axis=-1)
    return jnp.transpose(n, (0, 2, 1)).reshape(-1)


def _attention(q, k, v, fox=None, *, t=256, heads=4):
    b, s, d = q.shape
    w = heads * LANES
    qo_spec = pl.BlockSpec((1, t, w), lambda bi, h, i, *_: (bi, i, h))
    kv_spec = pl.BlockSpec((1, s, w), lambda bi, h, i, *_: (bi, 0, h))
    grid = (b, N_HEADS // heads, s // t)
    out_shape = jax.ShapeDtypeStruct((b, s, d), BF16)
    sem = ("parallel", "parallel", "arbitrary")
    if fox is not None:
        c, aug_q, aug_k, score_bound = fox
        counts = _fox_block_counts(c, score_bound, t, heads)
        return pl.pallas_call(
            functools.partial(_fox_attn_kernel, heads=heads),
            grid_spec=pltpu.PrefetchScalarGridSpec(
                num_scalar_prefetch=1,
                grid=grid,
                in_specs=[qo_spec, qo_spec, kv_spec, kv_spec, kv_spec],
                out_specs=qo_spec,
                scratch_shapes=[pltpu.VMEM((heads, t, LANES), F32),
                                pltpu.VMEM((heads, t, 2 * LANES), F32),
                                pltpu.VMEM((heads, t, t), F32)],
            ),
            out_shape=out_shape,
            compiler_params=_params(sem),
            name="fox_attn",
        )(counts, q, aug_q, k, aug_k, v)
    return pl.pallas_call(
        functools.partial(_sb_attn_kernel, heads=heads),
        grid=grid,
        in_specs=[qo_spec, kv_spec, kv_spec],
        out_specs=qo_spec,
        out_shape=out_shape,
        scratch_shapes=[pltpu.VMEM((heads, t, LANES), F32)] * 2,
        compiler_params=_params(sem),
        name="sb_attn",
    )(q, k, v)


def _oproj_kernel(m_ref, w_ref, x_ref, o_ref):
    o_ref[...] = x_ref[...] + jnp.dot(m_ref[...], w_ref[...], preferred_element_type=F32)


def _oproj(m2d, w, x2d, *, tm=512):
    t, d = x2d.shape
    row_spec = pl.BlockSpec((tm, d), lambda i: (i, 0))
    return pl.pallas_call(
        _oproj_kernel,
        grid=(t // tm,),
        in_specs=[row_spec, pl.BlockSpec((d, d), lambda i: (0, 0)), row_spec],
        out_specs=row_spec,
        out_shape=jax.ShapeDtypeStruct((t, d), F32),
        compiler_params=_params(("parallel",)),
        name="oproj",
    )(m2d, w, x2d)


def kernel(x, norm_g, ffn_w_in, ffn_w_out, fox_w_in, fox_b_f, fox_qk_g, sb_w_in, w_o):
    b, s, d = x.shape
    depth = norm_g.shape[0]
    head_dim = d // N_HEADS
    t = b * s
    x2d = x.reshape(t, d)
    for i in range(depth):
        g = norm_g[i]
        x2d = _ffn(x2d, g[0], ffn_w_in[i, 0], ffn_w_out[i, 0])
        j = i // N_MIXERS
        if i % N_MIXERS == 0:
            w = fox_w_in[j]
            w_f = jnp.pad(w[:, 3 * d:], ((0, 0), (0, LANES - N_HEADS))).astype(BF16)
            b_f = jnp.pad(fox_b_f[j], (0, LANES - N_HEADS)).reshape(1, LANES)
            g_q, g_k = fox_qk_g[j, 0:1], fox_qk_g[j, 1:2]
            q, k, v, lf = _proj(x2d, g[1], w[:, :3 * d].astype(BF16), (w_f, b_f, g_q, g_k))
            c, aug_q, aug_k = _cumsum_seq(lf.reshape(b, s, LANES), d)
            score_bound = (head_dim ** 0.5 * (1.0 + BF16_ULP) ** 2
                           * jnp.max(jnp.abs(g_q)) * jnp.max(jnp.abs(g_k)))
            m = _attention(q.reshape(b, s, d), k.reshape(b, s, d), v.reshape(b, s, d),
                           (c, aug_q, aug_k, score_bound))
        else:
            q, k, v = _proj(x2d, g[1], sb_w_in[j].astype(BF16))
            m = _attention(q.reshape(b, s, d), k.reshape(b, s, d), v.reshape(b, s, d))
        x2d = _oproj(m.reshape(t, d), w_o[i].astype(BF16), x2d)
        x2d = _ffn(x2d, g[2], ffn_w_in[i, 1], ffn_w_out[i, 1])
    return x2d.reshape(b, s, d)
```

```python
import functools

import jax
import jax.numpy as jnp
from jax import lax
from jax.experimental import pallas as pl
from jax.experimental.pallas import tpu as pltpu

N_HEADS = 16
N_MIXERS = 2
EPS = 1e-6
FFN_RES = 0.5
LANES = 128
VMEM_LIMIT = 56 * 1024 * 1024
EXP_ZERO = -104.0
BF16_ULP = 2.0 ** -8

F32 = jnp.float32
BF16 = jnp.bfloat16


def _rms_norm_rows(x, g):
    ms = jnp.mean(x * x, axis=-1, keepdims=True)
    return x * lax.rsqrt(ms + EPS) * g


def _params(sem):
    return pltpu.CompilerParams(dimension_semantics=sem, vmem_limit_bytes=VMEM_LIMIT)


def _ffn_kernel(x_ref, g_ref, wg_ref, wu_ref, wo_ref, o_ref, h_ref):
    @pl.when(pl.program_id(1) == 0)
    def _():
        x = x_ref[...]
        h_ref[...] = _rms_norm_rows(x, g_ref[...]).astype(BF16)
        o_ref[...] = x

    h = h_ref[...]
    gate = jnp.dot(h, wg_ref[...], preferred_element_type=F32)
    up = jnp.dot(h, wu_ref[...], preferred_element_type=F32)
    act = ((gate * jax.nn.sigmoid(gate)) * (up * FFN_RES)).astype(BF16)
    o_ref[...] += jnp.dot(act, wo_ref[...], preferred_element_type=F32)


def _ffn(x2d, g, w_in_all, w_out_all, layer, which, *, tm=1024, tf=512):
    t, d = x2d.shape
    d_ff = w_out_all.shape[2]
    nf = d_ff // tf
    return pl.pallas_call(
        _ffn_kernel,
        grid=(t // tm, nf),
        in_specs=[
            pl.BlockSpec((tm, d), lambda i, f: (i, 0)),
            pl.BlockSpec((1, d), lambda i, f: (0, 0)),
            pl.BlockSpec((None, None, d, tf), lambda i, f: (layer, which, 0, f)),
            pl.BlockSpec((None, None, d, tf), lambda i, f: (layer, which, 0, nf + f)),
            pl.BlockSpec((None, None, tf, d), lambda i, f: (layer, which, f, 0)),
        ],
        out_specs=pl.BlockSpec((tm, d), lambda i, f: (i, 0)),
        out_shape=jax.ShapeDtypeStruct((t, d), F32),
        scratch_shapes=[pltpu.VMEM((tm, d), BF16)],
        compiler_params=_params(("parallel", "arbitrary")),
        name="ffn",
    )(x2d, g.reshape(1, d), w_in_all, w_in_all, w_out_all)


def _proj_kernel(x_ref, g_ref, wq_ref, wk_ref, wv_ref, *rest, fox, head_dim):
    if fox:
        wf_ref, bf_ref, gq_ref, gk_ref, q_ref, k_ref, v_ref, lf_ref, h_ref = rest
    else:
        q_ref, k_ref, v_ref, h_ref = rest
    scale = head_dim ** -0.5

    @pl.when(pl.program_id(1) == 0)
    def _():
        h = _rms_norm_rows(x_ref[...], g_ref[...]).astype(BF16)
        h_ref[...] = h
        if fox:
            f_logit = jnp.dot(h, wf_ref[...], preferred_element_type=F32) + bf_ref[...]
            lf_ref[...] = jax.nn.log_sigmoid(f_logit)

    h = h_ref[...]
    q = jnp.dot(h, wq_ref[...], preferred_element_type=F32)
    k = jnp.dot(h, wk_ref[...], preferred_element_type=F32)
    v = jnp.dot(h, wv_ref[...], preferred_element_type=F32)
    v_ref[...] = v.astype(BF16)
    if fox:
        for hh in range(q.shape[1] // head_dim):
            sl = slice(hh * head_dim, (hh + 1) * head_dim)
            q_ref[:, sl] = (_rms_norm_rows(q[:, sl], gq_ref[...]) * scale).astype(BF16)
            k_ref[:, sl] = _rms_norm_rows(k[:, sl], gk_ref[...]).astype(BF16)
    else:
        q_ref[...] = (q * scale).astype(BF16)
        k_ref[...] = k.astype(BF16)


def _proj(x2d, g, w_all, layer, fox_extra=None, *, tm=1024, tn=512):
    t, d = x2d.shape
    head_dim = d // N_HEADS
    nn = d // tn
    fox = fox_extra is not None
    in_specs = [
        pl.BlockSpec((tm, d), lambda i, n: (i, 0)),
        pl.BlockSpec((1, d), lambda i, n: (0, 0)),
        pl.BlockSpec((None, d, tn), lambda i, n: (layer, 0, n)),
        pl.BlockSpec((None, d, tn), lambda i, n: (layer, 0, nn + n)),
        pl.BlockSpec((None, d, tn), lambda i, n: (layer, 0, 2 * nn + n)),
    ]
    args = [x2d, g.reshape(1, d), w_all, w_all, w_all]
    out_specs = [pl.BlockSpec((tm, tn), lambda i, n: (i, n))] * 3
    out_shape = [jax.ShapeDtypeStruct((t, d), BF16)] * 3
    if fox:
        w_f, b_f, g_q, g_k = fox_extra
        in_specs += [
            pl.BlockSpec((d, LANES), lambda i, n: (0, 0)),
            pl.BlockSpec((1, LANES), lambda i, n: (0, 0)),
            pl.BlockSpec((1, head_dim), lambda i, n: (0, 0)),
            pl.BlockSpec((1, head_dim), lambda i, n: (0, 0)),
        ]
        args += [w_f, b_f, g_q, g_k]
        out_specs = out_specs + [pl.BlockSpec((tm, LANES), lambda i, n: (i, 0))]
        out_shape = out_shape + [jax.ShapeDtypeStruct((t, LANES), F32)]
    return pl.pallas_call(
        functools.partial(_proj_kernel, fox=fox, head_dim=head_dim),
        grid=(t // tm, nn),
        in_specs=in_specs,
        out_specs=out_specs,
        out_shape=out_shape,
        scratch_shapes=[pltpu.VMEM((tm, d), BF16)],
        compiler_params=_params(("parallel", "arbitrary")),
        name="proj_fox" if fox else "proj_sb",
    )(*args)


N_SPLIT = 3


def _split3(x):
    hi = x.astype(BF16)
    r = x - hi.astype(F32)
    mid = r.astype(BF16)
    lo = (r - mid.astype(F32)).astype(BF16)
    return hi, mid, lo


def _cumsum_kernel(lf_ref, c_ref, aq_ref, ak_ref, carry_ref):
    @pl.when(pl.program_id(1) == 0)
    def _():
        carry_ref[...] = jnp.zeros_like(carry_ref)

    tc = lf_ref.shape[1]
    d = aq_ref.shape[2]
    row = lax.broadcasted_iota(jnp.int32, (tc, tc), 0)
    col = lax.broadcasted_iota(jnp.int32, (tc, tc), 1)
    tri = (col <= row).astype(BF16)
    hi, mid, lo = _split3(lf_ref[0])
    c = (jnp.dot(tri, lo, preferred_element_type=F32)
         + jnp.dot(tri, mid, preferred_element_type=F32)
         + jnp.dot(tri, hi, preferred_element_type=F32)) + carry_ref[...]
    c_ref[0] = c
    carry_ref[...] = c[tc - 1:tc, :]

    lane_c = lax.broadcasted_iota(jnp.int32, (tc, LANES), 1)
    hi, mid, lo = (p.astype(F32) for p in _split3(jnp.where(lane_c < N_HEADS, c, 0.0)))
    packed = (hi + pltpu.roll(mid, N_HEADS, axis=1)
              + pltpu.roll(lo, 2 * N_HEADS, axis=1)).astype(BF16)
    src = lax.broadcasted_iota(jnp.int32, (LANES, d), 0)
    lane = lax.broadcasted_iota(jnp.int32, (LANES, d), 1)
    src_head = jnp.bitwise_and(src, N_HEADS - 1)
    src_part = lax.shift_right_logical(src, N_HEADS.bit_length() - 1)

    def scatter(offset):
        sel = (lane == src_head * LANES + offset + src_part) & (src_part < N_SPLIT)
        return jnp.dot(packed, sel.astype(BF16), preferred_element_type=F32)

    lane_in_head = lax.broadcasted_iota(jnp.int32, (tc, d), 1) % LANES
    ones_q = (lane_in_head < N_SPLIT).astype(F32)
    ones_k = ((lane_in_head >= N_SPLIT) & (lane_in_head < 2 * N_SPLIT)).astype(F32)
    aq_ref[0] = (scatter(N_SPLIT) + ones_q).astype(BF16)
    ak_ref[0] = (ones_k - scatter(0)).astype(BF16)


def _cumsum_seq(lf, d, *, tc=512):
    b, s, w = lf.shape
    aug_spec = pl.BlockSpec((1, tc, d), lambda i, j: (i, j, 0))
    return pl.pallas_call(
        _cumsum_kernel,
        grid=(b, s // tc),
        in_specs=[pl.BlockSpec((1, tc, w), lambda i, j: (i, j, 0))],
        out_specs=[pl.BlockSpec((1, tc, w), lambda i, j: (i, j, 0)), aug_spec, aug_spec],
        out_shape=[jax.ShapeDtypeStruct((b, s, w), F32),
                   jax.ShapeDtypeStruct((b, s, d), BF16),
                   jax.ShapeDtypeStruct((b, s, d), BF16)],
        scratch_shapes=[pltpu.VMEM((1, w), F32)],
        compiler_params=_params(("parallel", "arbitrary")),
        name="cumsum",
    )(lf)


def _lane_tile(x, n):
    return x if n == LANES else jnp.concatenate([x] * (n // LANES), axis=1)


def _causal_mask(t, strict):
    row = lax.broadcasted_iota(jnp.int32, (t, t), 0)
    col = lax.broadcasted_iota(jnp.int32, (t, t), 1)
    return (col < row) if strict else (col <= row)


def _qk(q, k):
    return lax.dot_general(q, k, (((1,), (1,)), ((), ())), preferred_element_type=F32)


def _head(g):
    return slice(g * LANES, (g + 1) * LANES)


def _fox_attn_kernel(order_ref, nb_ref, *refs, heads):
    del order_ref
    q_refs, aq_refs, k_refs, ak_refs, v_refs = (refs[n * heads:(n + 1) * heads] for n in range(5))
    o_ref, m_ref, acc_ref, s_ref = refs[5 * heads:]
    i = pl.program_id(2)
    t = o_ref.shape[1]
    step_id = (pl.program_id(0) * pl.num_programs(1) + pl.program_id(1)) * pl.num_programs(2) + i
    n_off = nb_ref[step_id]
    ones = jnp.ones((t, LANES), BF16)

    def score(j):
        keys = pl.ds(pl.multiple_of(j * t, t), t)
        for g in range(heads):
            qa = jnp.concatenate([q_refs[g][0], aq_refs[g][0]], axis=1)
            ka = jnp.concatenate([k_refs[g][0, keys, :], ak_refs[g][0, keys, :]], axis=1)
            s_ref[g] = _qk(qa, ka)

    def step(j, masked):
        s_cur = [s_ref[g] for g in range(heads)]
        score(jnp.maximum(j - 1, 0))
        keys = pl.ds(pl.multiple_of(j * t, t), t)
        for g in range(heads):
            s = s_cur[g]
            if masked:
                s = jnp.where(_causal_mask(t, strict=False), s, -jnp.inf)
            m_prev = m_ref[g]
            m_new = jnp.maximum(m_prev, jnp.max(s, axis=1, keepdims=True))
            p = jnp.exp(s - _lane_tile(m_new, t))
            alpha = jnp.exp(m_prev - m_new)
            va = jnp.concatenate([v_refs[g][0, keys, :], ones], axis=1)
            acc_ref[g] = _lane_tile(alpha, 2 * LANES) * acc_ref[g] + jnp.dot(
                p.astype(BF16), va, preferred_element_type=F32)
            m_ref[g] = m_new

    m_ref[...] = jnp.full_like(m_ref, -jnp.inf)
    acc_ref[...] = jnp.zeros_like(acc_ref)
    score(i)
    step(i, True)

    def body(n, carry):
        step(i - 1 - n, False)
        return carry

    lax.fori_loop(0, n_off, body, 0)
    for g in range(heads):
        acc = acc_ref[g]
        o_ref[0, :, _head(g)] = (acc[:, :LANES] / acc[:, LANES:]).astype(o_ref.dtype)


def _sb_attn_kernel(q_ref, k_ref, v_ref, o_ref, r_ref, acc_ref, *, heads):
    i = pl.program_id(2)
    t = q_ref.shape[1]
    row = lax.broadcasted_iota(jnp.int32, (2 * t, t), 0)
    col = lax.broadcasted_iota(jnp.int32, (2 * t, t), 1)
    u2 = (((row >= col) & (row < t)) | (row - t >= col)).astype(BF16)

    def tile(j, masked):
        keys = pl.ds(pl.multiple_of(j * t, t), t)
        mask = _causal_mask(t, strict=True) if masked else None
        zs = [_qk(q_ref[0, :, _head(g)], k_ref[0, keys, _head(g)]) for g in range(heads)]
        parts = []
        for z in zs:
            l1m = -(jnp.maximum(z, 0.0) + jnp.log(1.0 + jnp.exp(-jnp.abs(z))))
            if masked:
                l1m = jnp.where(mask, l1m, 0.0)
            hi = l1m.astype(BF16)
            lo = (l1m - hi.astype(F32)).astype(BF16)
            parts.append(jnp.concatenate([hi, lo], axis=1))
        incls = [jnp.dot(p, u2, preferred_element_type=F32) for p in parts]
        for g in range(heads):
            r_prev = r_ref[g]
            a = jnp.exp(zs[g] + incls[g] + _lane_tile(r_prev, t))
            if masked:
                a = jnp.where(mask, a, 0.0)
            acc_ref[g] += jnp.dot(a.astype(BF16), v_ref[0, keys, _head(g)],
                                  preferred_element_type=F32)
            r_ref[g] = r_prev + incls[g][:, 0:1]

    r_ref[...] = jnp.zeros_like(r_ref)
    acc_ref[...] = jnp.zeros_like(acc_ref)
    tile(i, True)

    def cond(carry):
        j, live = carry
        return jnp.logical_and(j >= 0, live)

    def body(carry):
        j, _ = carry
        tile(j, False)
        return j - 1, jnp.max(r_ref[...]) >= EXP_ZERO

    lax.while_loop(cond, body, (i - 1, jnp.max(r_ref[...]) >= EXP_ZERO))
    for g in range(heads):
        o_ref[0, :, _head(g)] = acc_ref[g].astype(o_ref.dtype)


def _fox_block_counts(c, order, score_bound, t, heads):
    b, s, _ = c.shape
    n_blk = s // t
    ch = c[:, :, :N_HEADS][:, :, order]
    c_first_q = ch[:, ::t, :]
    c_last_k = ch[:, t - 1::t, :]
    decay = c_first_q[:, :, None, :] - c_last_k[:, None, :, :]
    live = decay >= (EXP_ZERO - 1.0) - 2.0 * score_bound
    qi = jnp.arange(n_blk)[None, :, None, None]
    kj = jnp.arange(n_blk)[None, None, :, None]
    live = jnp.logical_and(live, kj < qi)
    n = jnp.sum(live, axis=2).astype(jnp.int32)
    n = n.reshape(b, n_blk, N_HEADS // heads, heads).max(axis=-1)
    return jnp.transpose(n, (0, 2, 1)).reshape(-1)


def _fox_attention(q, k, v, c, aug_q, aug_k, order, score_bound, *, t=256, heads=4):
    b, s, d = q.shape
    counts = _fox_block_counts(c, order, score_bound, t, heads)

    def q_spec(g):
        return pl.BlockSpec((1, t, LANES),
                            lambda bi, h, i, order, nb: (bi, i, order[h * heads + g]))

    def kv_spec(g):
        return pl.BlockSpec((1, s, LANES),
                            lambda bi, h, i, order, nb: (bi, 0, order[h * heads + g]))

    q_specs = [q_spec(g) for g in range(heads)]
    kv_specs = [kv_spec(g) for g in range(heads)]
    return pl.pallas_call(
        functools.partial(_fox_attn_kernel, heads=heads),
        grid_spec=pltpu.PrefetchScalarGridSpec(
            num_scalar_prefetch=2,
            grid=(b, N_HEADS // heads, s // t),
            in_specs=q_specs + q_specs + kv_specs + kv_specs + kv_specs,
            out_specs=pl.BlockSpec((1, t, heads * LANES),
                                   lambda bi, h, i, order, nb: (bi, i, h)),
            scratch_shapes=[pltpu.VMEM((heads, t, LANES), F32),
                            pltpu.VMEM((heads, t, 2 * LANES), F32),
                            pltpu.VMEM((heads, t, t), F32)],
        ),
        out_shape=jax.ShapeDtypeStruct((b, s, d), BF16),
        compiler_params=_params(("parallel", "parallel", "arbitrary")),
        name="fox_attn",
    )(order, counts,
      *([q] * heads + [aug_q] * heads + [k] * heads + [aug_k] * heads + [v] * heads))


def _sb_attention(q, k, v, *, t=256, heads=4):
    b, s, d = q.shape
    w = heads * LANES
    qo_spec = pl.BlockSpec((1, t, w), lambda bi, h, i: (bi, i, h))
    kv_spec = pl.BlockSpec((1, s, w), lambda bi, h, i: (bi, 0, h))
    return pl.pallas_call(
        functools.partial(_sb_attn_kernel, heads=heads),
        grid=(b, N_HEADS // heads, s // t),
        in_specs=[qo_spec, kv_spec, kv_spec],
        out_specs=qo_spec,
        out_shape=jax.ShapeDtypeStruct((b, s, d), BF16),
        scratch_shapes=[pltpu.VMEM((heads, t, LANES), F32)] * 2,
        compiler_params=_params(("parallel", "parallel", "arbitrary")),
        name="sb_attn",
    )(q, k, v)


def _oproj_kernel(m_ref, w_ref, x_ref, o_ref):
    o_ref[...] = x_ref[...] + jnp.dot(m_ref[...], w_ref[...], preferred_element_type=F32)


def _oproj(m2d, w, x2d, *, tm=512):
    t, d = x2d.shape
    row_spec = pl.BlockSpec((tm, d), lambda i: (i, 0))
    return pl.pallas_call(
        _oproj_kernel,
        grid=(t // tm,),
        in_specs=[row_spec, pl.BlockSpec((d, d), lambda i: (0, 0)), row_spec],
        out_specs=row_spec,
        out_shape=jax.ShapeDtypeStruct((t, d), F32),
        compiler_params=_params(("parallel",)),
        name="oproj",
    )(m2d, w, x2d)


def kernel(x, norm_g, ffn_w_in, ffn_w_out, fox_w_in, fox_b_f, fox_qk_g, sb_w_in, w_o):
    b, s, d = x.shape
    depth = norm_g.shape[0]
    head_dim = d // N_HEADS
    t = b * s
    x2d = x.reshape(t, d)
    ffn_w_in, ffn_w_out = ffn_w_in.astype(BF16), ffn_w_out.astype(BF16)
    fox_w, sb_w, w_o = fox_w_in.astype(BF16), sb_w_in.astype(BF16), w_o.astype(BF16)
    for i in range(depth):
        g = norm_g[i]
        x2d = _ffn(x2d, g[0], ffn_w_in, ffn_w_out, i, 0)
        j = i // N_MIXERS
        if i % N_MIXERS == 0:
            w_f = jnp.pad(fox_w[j, :, 3 * d:], ((0, 0), (0, LANES - N_HEADS)))
            b_f = jnp.pad(fox_b_f[j], (0, LANES - N_HEADS)).reshape(1, LANES)
            g_q, g_k = fox_qk_g[j, 0:1], fox_qk_g[j, 1:2]
            q, k, v, lf = _proj(x2d, g[1], fox_w, j, (w_f, b_f, g_q, g_k))
            c, aug_q, aug_k = _cumsum_seq(lf.reshape(b, s, LANES), d)
            score_bound = (head_dim ** 0.5 * (1.0 + BF16_ULP) ** 2
                           * jnp.max(jnp.abs(g_q)) * jnp.max(jnp.abs(g_k)))
            order = jnp.argsort(jnp.sum(c[:, s - 1, :N_HEADS], axis=0)).astype(jnp.int32)
            m = _fox_attention(q.reshape(b, s, d), k.reshape(b, s, d), v.reshape(b, s, d),
                               c, aug_q, aug_k, order, score_bound)
            w_out_proj = w_o[i].reshape(N_HEADS, head_dim, d)[order].reshape(d, d)
        else:
            q, k, v = _proj(x2d, g[1], sb_w, j)
            m = _sb_attention(q.reshape(b, s, d), k.reshape(b, s, d), v.reshape(b, s, d))
            w_out_proj = w_o[i]
        x2d = _oproj(m.reshape(t, d), w_out_proj, x2d)
        x2d = _ffn(x2d, g[2], ffn_w_in, ffn_w_out, i, 1)
    return x2d.reshape(b, s, d)
```

```python
import functools

import jax
import jax.numpy as jnp
from jax import lax
from jax.experimental import pallas as pl
from jax.experimental.pallas import tpu as pltpu

N_HEADS = 16
N_MIXERS = 2
EPS = 1e-6
FFN_RES = 0.5
LANES = 128
VMEM_LIMIT = 56 * 1024 * 1024
EXP_ZERO = -104.0
BF16_ULP = 2.0 ** -8
LOG2E = 1.4426950408889634

F32 = jnp.float32
BF16 = jnp.bfloat16


def _rms_norm_rows(x, g):
    ms = jnp.mean(x * x, axis=-1, keepdims=True)
    return x * lax.rsqrt(ms + EPS) * g


def _params(sem):
    return pltpu.CompilerParams(dimension_semantics=sem, vmem_limit_bytes=VMEM_LIMIT)


def _ffn_kernel(x_ref, g_ref, wg_ref, wu_ref, wo_ref, o_ref, h_ref):
    @pl.when(pl.program_id(1) == 0)
    def _():
        x = x_ref[...]
        h_ref[...] = _rms_norm_rows(x, g_ref[...]).astype(BF16)
        o_ref[...] = x

    h = h_ref[...]
    gate = jnp.dot(h, wg_ref[...], preferred_element_type=F32)
    up = jnp.dot(h, wu_ref[...], preferred_element_type=F32)
    act = ((gate * jax.nn.sigmoid(gate)) * (up * FFN_RES)).astype(BF16)
    o_ref[...] += jnp.dot(act, wo_ref[...], preferred_element_type=F32)


def _ffn(x2d, g, w_in_all, w_out_all, layer, which, *, tm=1024, tf=512):
    t, d = x2d.shape
    d_ff = w_out_all.shape[2]
    nf = d_ff // tf
    return pl.pallas_call(
        _ffn_kernel,
        grid=(t // tm, nf),
        in_specs=[
            pl.BlockSpec((tm, d), lambda i, f: (i, 0)),
            pl.BlockSpec((1, d), lambda i, f: (0, 0)),
            pl.BlockSpec((None, None, d, tf), lambda i, f: (layer, which, 0, f)),
            pl.BlockSpec((None, None, d, tf), lambda i, f: (layer, which, 0, nf + f)),
            pl.BlockSpec((None, None, tf, d), lambda i, f: (layer, which, f, 0)),
        ],
        out_specs=pl.BlockSpec((tm, d), lambda i, f: (i, 0)),
        out_shape=jax.ShapeDtypeStruct((t, d), F32),
        scratch_shapes=[pltpu.VMEM((tm, d), BF16)],
        compiler_params=_params(("parallel", "arbitrary")),
        name="ffn",
    )(x2d, g.reshape(1, d), w_in_all, w_in_all, w_out_all)


def _proj_kernel(x_ref, g_ref, wq_ref, wk_ref, wv_ref, *rest, fox, head_dim):
    if fox:
        wf_ref, bf_ref, gq_ref, gk_ref, q_ref, k_ref, v_ref, lf_ref, h_ref = rest
    else:
        q_ref, k_ref, v_ref, h_ref = rest
    scale = head_dim ** -0.5

    @pl.when(pl.program_id(1) == 0)
    def _():
        h = _rms_norm_rows(x_ref[...], g_ref[...]).astype(BF16)
        h_ref[...] = h
        if fox:
            f_logit = jnp.dot(h, wf_ref[...], preferred_element_type=F32) + bf_ref[...]
            lf_ref[...] = jax.nn.log_sigmoid(f_logit)

    h = h_ref[...]
    q = jnp.dot(h, wq_ref[...], preferred_element_type=F32)
    k = jnp.dot(h, wk_ref[...], preferred_element_type=F32)
    v = jnp.dot(h, wv_ref[...], preferred_element_type=F32)
    v_ref[...] = v.astype(BF16)
    if fox:
        for hh in range(q.shape[1] // head_dim):
            sl = slice(hh * head_dim, (hh + 1) * head_dim)
            q_ref[:, sl] = (_rms_norm_rows(q[:, sl], gq_ref[...]) * scale).astype(BF16)
            k_ref[:, sl] = _rms_norm_rows(k[:, sl], gk_ref[...]).astype(BF16)
    else:
        q_ref[...] = (q * (scale * LOG2E)).astype(BF16)
        k_ref[...] = k.astype(BF16)


def _proj(x2d, g, w_all, layer, fox_extra=None, *, tm=1024, tn=512):
    t, d = x2d.shape
    head_dim = d // N_HEADS
    nn = d // tn
    fox = fox_extra is not None
    in_specs = [
        pl.BlockSpec((tm, d), lambda i, n: (i, 0)),
        pl.BlockSpec((1, d), lambda i, n: (0, 0)),
        pl.BlockSpec((None, d, tn), lambda i, n: (layer, 0, n)),
        pl.BlockSpec((None, d, tn), lambda i, n: (layer, 0, nn + n)),
        pl.BlockSpec((None, d, tn), lambda i, n: (layer, 0, 2 * nn + n)),
    ]
    args = [x2d, g.reshape(1, d), w_all, w_all, w_all]
    out_specs = [pl.BlockSpec((tm, tn), lambda i, n: (i, n))] * 3
    out_shape = [jax.ShapeDtypeStruct((t, d), BF16)] * 3
    if fox:
        w_f, b_f, g_q, g_k = fox_extra
        in_specs += [
            pl.BlockSpec((d, LANES), lambda i, n: (0, 0)),
            pl.BlockSpec((1, LANES), lambda i, n: (0, 0)),
            pl.BlockSpec((1, head_dim), lambda i, n: (0, 0)),
            pl.BlockSpec((1, head_dim), lambda i, n: (0, 0)),
        ]
        args += [w_f, b_f, g_q, g_k]
        out_specs = out_specs + [pl.BlockSpec((tm, LANES), lambda i, n: (i, 0))]
        out_shape = out_shape + [jax.ShapeDtypeStruct((t, LANES), F32)]
    return pl.pallas_call(
        functools.partial(_proj_kernel, fox=fox, head_dim=head_dim),
        grid=(t // tm, nn),
        in_specs=in_specs,
        out_specs=out_specs,
        out_shape=out_shape,
        scratch_shapes=[pltpu.VMEM((tm, d), BF16)],
        compiler_params=_params(("parallel", "arbitrary")),
        name="proj_fox" if fox else "proj_sb",
    )(*args)


N_SPLIT = 3


def _split3(x):
    hi = x.astype(BF16)
    r = x - hi.astype(F32)
    mid = r.astype(BF16)
    lo = (r - mid.astype(F32)).astype(BF16)
    return hi, mid, lo


def _cumsum_kernel(lf_ref, c_ref, aq_ref, ak_ref, carry_ref):
    @pl.when(pl.program_id(1) == 0)
    def _():
        carry_ref[...] = jnp.zeros_like(carry_ref)

    tc = lf_ref.shape[1]
    d = aq_ref.shape[2]
    row = lax.broadcasted_iota(jnp.int32, (tc, tc), 0)
    col = lax.broadcasted_iota(jnp.int32, (tc, tc), 1)
    tri = (col <= row).astype(BF16)
    hi, mid, lo = _split3(lf_ref[0])
    c = (jnp.dot(tri, lo, preferred_element_type=F32)
         + jnp.dot(tri, mid, preferred_element_type=F32)
         + jnp.dot(tri, hi, preferred_element_type=F32)) + carry_ref[...]
    c_ref[0] = c
    carry_ref[...] = c[tc - 1:tc, :]

    lane_c = lax.broadcasted_iota(jnp.int32, (tc, LANES), 1)
    hi, mid, lo = (p.astype(F32) for p in _split3(jnp.where(lane_c < N_HEADS, c, 0.0)))
    packed = (hi + pltpu.roll(mid, N_HEADS, axis=1)
              + pltpu.roll(lo, 2 * N_HEADS, axis=1)).astype(BF16)
    src = lax.broadcasted_iota(jnp.int32, (LANES, d), 0)
    lane = lax.broadcasted_iota(jnp.int32, (LANES, d), 1)
    src_head = jnp.bitwise_and(src, N_HEADS - 1)
    src_part = lax.shift_right_logical(src, N_HEADS.bit_length() - 1)

    def scatter(offset):
        sel = (lane == src_head * LANES + offset + src_part) & (src_part < N_SPLIT)
        return jnp.dot(packed, sel.astype(BF16), preferred_element_type=F32)

    lane_in_head = lax.broadcasted_iota(jnp.int32, (tc, d), 1) % LANES
    ones_q = (lane_in_head < N_SPLIT).astype(F32)
    ones_k = ((lane_in_head >= N_SPLIT) & (lane_in_head < 2 * N_SPLIT)).astype(F32)
    aq_ref[0] = (scatter(N_SPLIT) + ones_q).astype(BF16)
    ak_ref[0] = (ones_k - scatter(0)).astype(BF16)


def _cumsum_seq(lf, d, *, tc=512):
    b, s, w = lf.shape
    aug_spec = pl.BlockSpec((1, tc, d), lambda i, j: (i, j, 0))
    return pl.pallas_call(
        _cumsum_kernel,
        grid=(b, s // tc),
        in_specs=[pl.BlockSpec((1, tc, w), lambda i, j: (i, j, 0))],
        out_specs=[pl.BlockSpec((1, tc, w), lambda i, j: (i, j, 0)), aug_spec, aug_spec],
        out_shape=[jax.ShapeDtypeStruct((b, s, w), F32),
                   jax.ShapeDtypeStruct((b, s, d), BF16),
                   jax.ShapeDtypeStruct((b, s, d), BF16)],
        scratch_shapes=[pltpu.VMEM((1, w), F32)],
        compiler_params=_params(("parallel", "arbitrary")),
        name="cumsum",
    )(lf)


def _lane_tile(x, n):
    return x if n == LANES else jnp.concatenate([x] * (n // LANES), axis=1)


def _causal_mask(t, strict):
    row = lax.broadcasted_iota(jnp.int32, (t, t), 0)
    col = lax.broadcasted_iota(jnp.int32, (t, t), 1)
    return (col < row) if strict else (col <= row)


def _qk(q, k):
    return lax.dot_general(q, k, (((1,), (1,)), ((), ())), preferred_element_type=F32)


def _head(g):
    return slice(g * LANES, (g + 1) * LANES)


def _fox_attn_kernel(order_ref, nb_ref, *refs, heads):
    del order_ref
    q_refs, aq_refs, k_refs, ak_refs, v_refs = (refs[n * heads:(n + 1) * heads] for n in range(5))
    o_ref, m_ref, acc_ref, s_ref = refs[5 * heads:]
    i = pl.program_id(2)
    t = o_ref.shape[1]
    step_id = (pl.program_id(0) * pl.num_programs(1) + pl.program_id(1)) * pl.num_programs(2) + i
    n_off = nb_ref[step_id]
    ones = jnp.ones((t, LANES), BF16)

    def score(j):
        keys = pl.ds(pl.multiple_of(j * t, t), t)
        for g in range(heads):
            qa = jnp.concatenate([q_refs[g][0], aq_refs[g][0]], axis=1)
            ka = jnp.concatenate([k_refs[g][0, keys, :], ak_refs[g][0, keys, :]], axis=1)
            s_ref[g] = _qk(qa, ka)

    def step(j, masked):
        s_cur = [s_ref[g] for g in range(heads)]
        score(jnp.maximum(j - 1, 0))
        keys = pl.ds(pl.multiple_of(j * t, t), t)
        for g in range(heads):
            s = s_cur[g]
            if masked:
                s = jnp.where(_causal_mask(t, strict=False), s, -jnp.inf)
            m_prev = m_ref[g]
            m_new = jnp.maximum(m_prev, jnp.max(s, axis=1, keepdims=True))
            p = jnp.exp(s - _lane_tile(m_new, t))
            alpha = jnp.exp(m_prev - m_new)
            va = jnp.concatenate([v_refs[g][0, keys, :], ones], axis=1)
            acc_ref[g] = _lane_tile(alpha, 2 * LANES) * acc_ref[g] + jnp.dot(
                p.astype(BF16), va, preferred_element_type=F32)
            m_ref[g] = m_new

    m_ref[...] = jnp.full_like(m_ref, -jnp.inf)
    acc_ref[...] = jnp.zeros_like(acc_ref)
    score(i)
    step(i, True)

    def body(n, carry):
        step(i - 1 - n, False)
        return carry

    lax.fori_loop(0, n_off, body, 0)
    for g in range(heads):
        acc = acc_ref[g]
        o_ref[0, :, _head(g)] = (acc[:, :LANES] / acc[:, LANES:]).astype(o_ref.dtype)


def _sb_attn_kernel(q_ref, k_ref, v_ref, o_ref, r_ref, acc_ref, *, heads):
    i = pl.program_id(2)
    t = q_ref.shape[1]
    row = lax.broadcasted_iota(jnp.int32, (2 * t, t), 0)
    col = lax.broadcasted_iota(jnp.int32, (2 * t, t), 1)
    u2 = (((row >= col) & (row < t)) | (row - t >= col)).astype(BF16)

    def tile(j, masked):
        keys = pl.ds(pl.multiple_of(j * t, t), t)
        mask = _causal_mask(t, strict=True) if masked else None
        ws = [_qk(q_ref[0, :, _head(g)], k_ref[0, keys, _head(g)]) for g in range(heads)]
        parts = []
        for w in ws:
            sp = jnp.maximum(w, 0.0) + jnp.log2(1.0 + jnp.exp2(-jnp.abs(w)))
            if masked:
                sp = jnp.where(mask, sp, 0.0)
            hi = sp.astype(BF16)
            lo = (sp - hi.astype(F32)).astype(BF16)
            parts.append(jnp.concatenate([hi, lo], axis=1))
        incls = [jnp.dot(p, u2, preferred_element_type=F32) for p in parts]
        for g in range(heads):
            r_prev = r_ref[g]
            a = jnp.exp2(ws[g] - incls[g] - _lane_tile(r_prev, t))
            if masked:
                a = jnp.where(mask, a, 0.0)
            acc_ref[g] += jnp.dot(a.astype(BF16), v_ref[0, keys, _head(g)],
                                  preferred_element_type=F32)
            r_ref[g] = r_prev + incls[g][:, 0:1]

    r_ref[...] = jnp.zeros_like(r_ref)
    acc_ref[...] = jnp.zeros_like(acc_ref)
    tile(i, True)

    zero_bits = -EXP_ZERO * LOG2E

    def cond(carry):
        j, live = carry
        return jnp.logical_and(j >= 0, live)

    def body(carry):
        j, _ = carry
        tile(j, False)
        return j - 1, jnp.min(r_ref[...]) <= zero_bits

    lax.while_loop(cond, body, (i - 1, jnp.min(r_ref[...]) <= zero_bits))
    for g in range(heads):
        o_ref[0, :, _head(g)] = acc_ref[g].astype(o_ref.dtype)


def _fox_block_counts(c, order, score_bound, t, heads):
    b, s, _ = c.shape
    n_blk = s // t
    ch = c[:, :, :N_HEADS][:, :, order]
    c_first_q = ch[:, ::t, :]
    c_last_k = ch[:, t - 1::t, :]
    decay = c_first_q[:, :, None, :] - c_last_k[:, None, :, :]
    live = decay >= (EXP_ZERO - 1.0) - 2.0 * score_bound
    qi = jnp.arange(n_blk)[None, :, None, None]
    kj = jnp.arange(n_blk)[None, None, :, None]
    live = jnp.logical_and(live, kj < qi)
    n = jnp.sum(live, axis=2).astype(jnp.int32)
    n = n.reshape(b, n_blk, N_HEADS // heads, heads).max(axis=-1)
    return jnp.transpose(n, (0, 2, 1)).reshape(-1)


def _fox_attention(q, k, v, c, aug_q, aug_k, order, score_bound, *, t=256, heads=4):
    b, s, d = q.shape
    counts = _fox_block_counts(c, order, score_bound, t, heads)

    def q_spec(g):
        return pl.BlockSpec((1, t, LANES),
                            lambda bi, h, i, order, nb: (bi, i, order[h * heads + g]))

    def kv_spec(g):
        return pl.BlockSpec((1, s, LANES),
                            lambda bi, h, i, order, nb: (bi, 0, order[h * heads + g]))

    q_specs = [q_spec(g) for g in range(heads)]
    kv_specs = [kv_spec(g) for g in range(heads)]
    return pl.pallas_call(
        functools.partial(_fox_attn_kernel, heads=heads),
        grid_spec=pltpu.PrefetchScalarGridSpec(
            num_scalar_prefetch=2,
            grid=(b, N_HEADS // heads, s // t),
            in_specs=q_specs + q_specs + kv_specs + kv_specs + kv_specs,
            out_specs=pl.BlockSpec((1, t, heads * LANES),
                                   lambda bi, h, i, order, nb: (bi, i, h)),
            scratch_shapes=[pltpu.VMEM((heads, t, LANES), F32),
                            pltpu.VMEM((heads, t, 2 * LANES), F32),
                            pltpu.VMEM((heads, t, t), F32)],
        ),
        out_shape=jax.ShapeDtypeStruct((b, s, d), BF16),
        compiler_params=_params(("parallel", "parallel", "arbitrary")),
        name="fox_attn",
    )(order, counts,
      *([q] * heads + [aug_q] * heads + [k] * heads + [aug_k] * heads + [v] * heads))


def _sb_attention(q, k, v, *, t=256, heads=8):
    b, s, d = q.shape
    w = heads * LANES
    qo_spec = pl.BlockSpec((1, t, w), lambda bi, h, i: (bi, i, h))
    kv_spec = pl.BlockSpec((1, s, w), lambda bi, h, i: (bi, 0, h))
    return pl.pallas_call(
        functools.partial(_sb_attn_kernel, heads=heads),
        grid=(b, N_HEADS // heads, s // t),
        in_specs=[qo_spec, kv_spec, kv_spec],
        out_specs=qo_spec,
        out_shape=jax.ShapeDtypeStruct((b, s, d), BF16),
        scratch_shapes=[pltpu.VMEM((heads, t, LANES), F32)] * 2,
        compiler_params=_params(("parallel", "parallel", "arbitrary")),
        name="sb_attn",
    )(q, k, v)


def _oproj_kernel(m_ref, w_ref, x_ref, o_ref):
    o_ref[...] = x_ref[...] + jnp.dot(m_ref[...], w_ref[...], preferred_element_type=F32)


def _oproj(m2d, w, x2d, *, tm=512):
    t, d = x2d.shape
    row_spec = pl.BlockSpec((tm, d), lambda i: (i, 0))
    return pl.pallas_call(
        _oproj_kernel,
        grid=(t // tm,),
        in_specs=[row_spec, pl.BlockSpec((d, d), lambda i: (0, 0)), row_spec],
        out_specs=row_spec,
        out_shape=jax.ShapeDtypeStruct((t, d), F32),
        compiler_params=_params(("parallel",)),
        name="oproj",
    )(m2d, w, x2d)


def kernel(x, norm_g, ffn_w_in, ffn_w_out, fox_w_in, fox_b_f, fox_qk_g, sb_w_in, w_o):
    b, s, d = x.shape
    depth = norm_g.shape[0]
    head_dim = d // N_HEADS
    t = b * s
    x2d = x.reshape(t, d)
    ffn_w_in, ffn_w_out = ffn_w_in.astype(BF16), ffn_w_out.astype(BF16)
    fox_w, sb_w, w_o = fox_w_in.astype(BF16), sb_w_in.astype(BF16), w_o.astype(BF16)
    for i in range(depth):
        g = norm_g[i]
        x2d = _ffn(x2d, g[0], ffn_w_in, ffn_w_out, i, 0)
        j = i // N_MIXERS
        if i % N_MIXERS == 0:
            w_f = jnp.pad(fox_w[j, :, 3 * d:], ((0, 0), (0, LANES - N_HEADS)))
            b_f = jnp.pad(fox_b_f[j], (0, LANES - N_HEADS)).reshape(1, LANES)
            g_q, g_k = fox_qk_g[j, 0:1], fox_qk_g[j, 1:2]
            q, k, v, lf = _proj(x2d, g[1], fox_w, j, (w_f, b_f, g_q, g_k))
            c, aug_q, aug_k = _cumsum_seq(lf.reshape(b, s, LANES), d)
            score_bound = (head_dim ** 0.5 * (1.0 + BF16_ULP) ** 2
                           * jnp.max(jnp.abs(g_q)) * jnp.max(jnp.abs(g_k)))
            order = jnp.argsort(jnp.sum(c[:, s - 1, :N_HEADS], axis=0)).astype(jnp.int32)
            m = _fox_attention(q.reshape(b, s, d), k.reshape(b, s, d), v.reshape(b, s, d),
                               c, aug_q, aug_k, order, score_bound)
            w_out_proj = w_o[i].reshape(N_HEADS, head_dim, d)[order].reshape(d, d)
        else:
            q, k, v = _proj(x2d, g[1], sb_w, j)
            m = _sb_attention(q.reshape(b, s, d), k.reshape(b, s, d), v.reshape(b, s, d))
            w_out_proj = w_o[i]
        x2d = _oproj(m.reshape(t, d), w_out_proj, x2d)
        x2d = _ffn(x2d, g[2], ffn_w_in, ffn_w_out, i, 1)
    return x2d.reshape(b, s, d)
```

```python
import functools

import jax
import jax.numpy as jnp
from jax import lax
from jax.experimental import pallas as pl
from jax.experimental.pallas import tpu as pltpu

N_HEADS = 16
N_MIXERS = 2
EPS = 1e-6
FFN_RES = 0.5
LANES = 128
VMEM_LIMIT = 56 * 1024 * 1024
EXP_ZERO = -104.0
BF16_ULP = 2.0 ** -8
LOG2E = 1.4426950408889634

F32 = jnp.float32
BF16 = jnp.bfloat16


def _rms_norm_rows(x, g):
    ms = jnp.mean(x * x, axis=-1, keepdims=True)
    return x * lax.rsqrt(ms + EPS) * g


def _params(sem):
    return pltpu.CompilerParams(dimension_semantics=sem, vmem_limit_bytes=VMEM_LIMIT)


def _ffn_kernel(x_ref, g_ref, wg_ref, wu_ref, wo_ref, o_ref, h_ref):
    @pl.when(pl.program_id(1) == 0)
    def _():
        x = x_ref[...]
        h_ref[...] = _rms_norm_rows(x, g_ref[...]).astype(BF16)
        o_ref[...] = x

    h = h_ref[...]
    gate = jnp.dot(h, wg_ref[...], preferred_element_type=F32)
    up = jnp.dot(h, wu_ref[...], preferred_element_type=F32)
    act = ((gate * jax.nn.sigmoid(gate)) * (up * FFN_RES)).astype(BF16)
    o_ref[...] += jnp.dot(act, wo_ref[...], preferred_element_type=F32)


def _ffn(x2d, g, w_in_all, w_out_all, layer, which, *, tm=1024, tf=512):
    t, d = x2d.shape
    d_ff = w_out_all.shape[2]
    nf = d_ff // tf
    return pl.pallas_call(
        _ffn_kernel,
        grid=(t // tm, nf),
        in_specs=[
            pl.BlockSpec((tm, d), lambda i, f: (i, 0)),
            pl.BlockSpec((1, d), lambda i, f: (0, 0)),
            pl.BlockSpec((None, None, d, tf), lambda i, f: (layer, which, 0, f)),
            pl.BlockSpec((None, None, d, tf), lambda i, f: (layer, which, 0, nf + f)),
            pl.BlockSpec((None, None, tf, d), lambda i, f: (layer, which, f, 0)),
        ],
        out_specs=pl.BlockSpec((tm, d), lambda i, f: (i, 0)),
        out_shape=jax.ShapeDtypeStruct((t, d), F32),
        scratch_shapes=[pltpu.VMEM((tm, d), BF16)],
        compiler_params=_params(("parallel", "arbitrary")),
        name="ffn",
    )(x2d, g.reshape(1, d), w_in_all, w_in_all, w_out_all)


def _proj_kernel(x_ref, g_ref, wq_ref, wk_ref, wv_ref, *rest, fox, head_dim):
    if fox:
        wf_ref, bf_ref, gq_ref, gk_ref, q_ref, k_ref, v_ref, lf_ref, h_ref = rest
    else:
        q_ref, k_ref, v_ref, h_ref = rest
    scale = head_dim ** -0.5

    @pl.when(pl.program_id(1) == 0)
    def _():
        h = _rms_norm_rows(x_ref[...], g_ref[...]).astype(BF16)
        h_ref[...] = h
        if fox:
            f_logit = jnp.dot(h, wf_ref[...], preferred_element_type=F32) + bf_ref[...]
            lf_ref[...] = jax.nn.log_sigmoid(f_logit)

    h = h_ref[...]
    q = jnp.dot(h, wq_ref[...], preferred_element_type=F32)
    k = jnp.dot(h, wk_ref[...], preferred_element_type=F32)
    v = jnp.dot(h, wv_ref[...], preferred_element_type=F32)
    v_ref[...] = v.astype(BF16)
    if fox:
        for hh in range(q.shape[1] // head_dim):
            sl = slice(hh * head_dim, (hh + 1) * head_dim)
            q_ref[:, sl] = (_rms_norm_rows(q[:, sl], gq_ref[...]) * scale).astype(BF16)
            k_ref[:, sl] = _rms_norm_rows(k[:, sl], gk_ref[...]).astype(BF16)
    else:
        q_ref[...] = (q * (scale * LOG2E)).astype(BF16)
        k_ref[...] = k.astype(BF16)


def _proj(x2d, g, w_all, layer, fox_extra=None, *, tm=1024, tn=512):
    t, d = x2d.shape
    head_dim = d // N_HEADS
    nn = d // tn
    fox = fox_extra is not None
    in_specs = [
        pl.BlockSpec((tm, d), lambda i, n: (i, 0)),
        pl.BlockSpec((1, d), lambda i, n: (0, 0)),
        pl.BlockSpec((None, d, tn), lambda i, n: (layer, 0, n)),
        pl.BlockSpec((None, d, tn), lambda i, n: (layer, 0, nn + n)),
        pl.BlockSpec((None, d, tn), lambda i, n: (layer, 0, 2 * nn + n)),
    ]
    args = [x2d, g.reshape(1, d), w_all, w_all, w_all]
    out_specs = [pl.BlockSpec((tm, tn), lambda i, n: (i, n))] * 3
    out_shape = [jax.ShapeDtypeStruct((t, d), BF16)] * 3
    if fox:
        w_f, b_f, g_q, g_k = fox_extra
        in_specs += [
            pl.BlockSpec((d, LANES), lambda i, n: (0, 0)),
            pl.BlockSpec((1, LANES), lambda i, n: (0, 0)),
            pl.BlockSpec((1, head_dim), lambda i, n: (0, 0)),
            pl.BlockSpec((1, head_dim), lambda i, n: (0, 0)),
        ]
        args += [w_f, b_f, g_q, g_k]
        out_specs = out_specs + [pl.BlockSpec((tm, LANES), lambda i, n: (i, 0))]
        out_shape = out_shape + [jax.ShapeDtypeStruct((t, LANES), F32)]
    return pl.pallas_call(
        functools.partial(_proj_kernel, fox=fox, head_dim=head_dim),
        grid=(t // tm, nn),
        in_specs=in_specs,
        out_specs=out_specs,
        out_shape=out_shape,
        scratch_shapes=[pltpu.VMEM((tm, d), BF16)],
        compiler_params=_params(("parallel", "arbitrary")),
        name="proj_fox" if fox else "proj_sb",
    )(*args)


N_SPLIT = 3


def _split3(x):
    hi = x.astype(BF16)
    r = x - hi.astype(F32)
    mid = r.astype(BF16)
    lo = (r - mid.astype(F32)).astype(BF16)
    return hi, mid, lo


def _cumsum_kernel(lf_ref, c_ref, aq_ref, ak_ref, carry_ref):
    @pl.when(pl.program_id(1) == 0)
    def _():
        carry_ref[...] = jnp.zeros_like(carry_ref)

    tc = lf_ref.shape[1]
    d = aq_ref.shape[2]
    row = lax.broadcasted_iota(jnp.int32, (tc, tc), 0)
    col = lax.broadcasted_iota(jnp.int32, (tc, tc), 1)
    tri = (col <= row).astype(BF16)
    hi, mid, lo = _split3(lf_ref[0])
    c = (jnp.dot(tri, lo, preferred_element_type=F32)
         + jnp.dot(tri, mid, preferred_element_type=F32)
         + jnp.dot(tri, hi, preferred_element_type=F32)) + carry_ref[...]
    c_ref[0] = c
    carry_ref[...] = c[tc - 1:tc, :]

    lane_c = lax.broadcasted_iota(jnp.int32, (tc, LANES), 1)
    hi, mid, lo = (p.astype(F32) for p in _split3(jnp.where(lane_c < N_HEADS, c, 0.0)))
    packed = (hi + pltpu.roll(mid, N_HEADS, axis=1)
              + pltpu.roll(lo, 2 * N_HEADS, axis=1)).astype(BF16)
    src = lax.broadcasted_iota(jnp.int32, (LANES, d), 0)
    lane = lax.broadcasted_iota(jnp.int32, (LANES, d), 1)
    src_head = jnp.bitwise_and(src, N_HEADS - 1)
    src_part = lax.shift_right_logical(src, N_HEADS.bit_length() - 1)

    def scatter(offset):
        sel = (lane == src_head * LANES + offset + src_part) & (src_part < N_SPLIT)
        return jnp.dot(packed, sel.astype(BF16), preferred_element_type=F32)

    lane_in_head = lax.broadcasted_iota(jnp.int32, (tc, d), 1) % LANES
    ones_q = (lane_in_head < N_SPLIT).astype(F32)
    ones_k = ((lane_in_head >= N_SPLIT) & (lane_in_head < 2 * N_SPLIT)).astype(F32)
    aq_ref[0] = (scatter(N_SPLIT) + ones_q).astype(BF16)
    ak_ref[0] = (ones_k - scatter(0)).astype(BF16)


def _cumsum_seq(lf, d, *, tc=512):
    b, s, w = lf.shape
    aug_spec = pl.BlockSpec((1, tc, d), lambda i, j: (i, j, 0))
    return pl.pallas_call(
        _cumsum_kernel,
        grid=(b, s // tc),
        in_specs=[pl.BlockSpec((1, tc, w), lambda i, j: (i, j, 0))],
        out_specs=[pl.BlockSpec((1, tc, w), lambda i, j: (i, j, 0)), aug_spec, aug_spec],
        out_shape=[jax.ShapeDtypeStruct((b, s, w), F32),
                   jax.ShapeDtypeStruct((b, s, d), BF16),
                   jax.ShapeDtypeStruct((b, s, d), BF16)],
        scratch_shapes=[pltpu.VMEM((1, w), F32)],
        compiler_params=_params(("parallel", "arbitrary")),
        name="cumsum",
    )(lf)


def _lane_tile(x, n):
    return x if n == LANES else jnp.concatenate([x] * (n // LANES), axis=1)


def _causal_mask(t, strict):
    row = lax.broadcasted_iota(jnp.int32, (t, t), 0)
    col = lax.broadcasted_iota(jnp.int32, (t, t), 1)
    return (col < row) if strict else (col <= row)


def _qk(q, k):
    return lax.dot_general(q, k, (((1,), (1,)), ((), ())), preferred_element_type=F32)


def _head(g):
    return slice(g * LANES, (g + 1) * LANES)


def _fox_attn_kernel(order_ref, nb_ref, *refs, heads):
    del order_ref
    q_refs, aq_refs, k_refs, ak_refs, v_refs = (refs[n * heads:(n + 1) * heads] for n in range(5))
    o_ref, m_ref, acc_ref, s_ref = refs[5 * heads:]
    i = pl.program_id(2)
    t = o_ref.shape[1]
    step_id = (pl.program_id(0) * pl.num_programs(1) + pl.program_id(1)) * pl.num_programs(2) + i
    n_off = nb_ref[step_id]
    ones = jnp.ones((t, LANES), BF16)

    def score(j):
        keys = pl.ds(pl.multiple_of(j * t, t), t)
        for g in range(heads):
            qa = jnp.concatenate([q_refs[g][0], aq_refs[g][0]], axis=1)
            ka = jnp.concatenate([k_refs[g][0, keys, :], ak_refs[g][0, keys, :]], axis=1)
            s_ref[g] = _qk(qa, ka)

    def step(j, masked):
        s_cur = [s_ref[g] for g in range(heads)]
        score(jnp.maximum(j - 1, 0))
        keys = pl.ds(pl.multiple_of(j * t, t), t)
        for g in range(heads):
            s = s_cur[g]
            if masked:
                s = jnp.where(_causal_mask(t, strict=False), s, -jnp.inf)
            m_prev = m_ref[g]
            m_new = jnp.maximum(m_prev, jnp.max(s, axis=1, keepdims=True))
            p = jnp.exp(s - _lane_tile(m_new, t))
            alpha = jnp.exp(m_prev - m_new)
            va = jnp.concatenate([v_refs[g][0, keys, :], ones], axis=1)
            acc_ref[g] = _lane_tile(alpha, 2 * LANES) * acc_ref[g] + jnp.dot(
                p.astype(BF16), va, preferred_element_type=F32)
            m_ref[g] = m_new

    m_ref[...] = jnp.full_like(m_ref, -jnp.inf)
    acc_ref[...] = jnp.zeros_like(acc_ref)
    score(i)
    step(i, True)

    def body(n, carry):
        step(i - 1 - n, False)
        return carry

    lax.fori_loop(0, n_off, body, 0)
    for g in range(heads):
        acc = acc_ref[g]
        o_ref[0, :, _head(g)] = (acc[:, :LANES] / acc[:, LANES:]).astype(o_ref.dtype)


def _sb_attn_kernel(q_ref, k_ref, v_ref, o_ref, r_ref, acc_ref, *, heads):
    i = pl.program_id(2)
    t = q_ref.shape[1]
    row = lax.broadcasted_iota(jnp.int32, (2 * t, t), 0)
    col = lax.broadcasted_iota(jnp.int32, (2 * t, t), 1)
    u2 = (((row >= col) & (row < t)) | (row - t >= col)).astype(BF16)

    def tile(j, masked):
        keys = pl.ds(pl.multiple_of(j * t, t), t)
        mask = _causal_mask(t, strict=True) if masked else None
        ws = [_qk(q_ref[0, :, _head(g)], k_ref[0, keys, _head(g)]) for g in range(heads)]
        parts = []
        for w in ws:
            sp = jnp.maximum(w, 0.0) + jnp.log2(1.0 + jnp.exp2(-jnp.abs(w)))
            if masked:
                sp = jnp.where(mask, sp, 0.0)
            hi = sp.astype(BF16)
            lo = (sp - hi.astype(F32)).astype(BF16)
            parts.append(jnp.concatenate([hi, lo], axis=1))
        incls = [jnp.dot(p, u2, preferred_element_type=F32) for p in parts]
        for g in range(heads):
            r_prev = r_ref[g]
            a = jnp.exp2(ws[g] - incls[g] - _lane_tile(r_prev, t))
            if masked:
                a = jnp.where(mask, a, 0.0)
            acc_ref[g] += jnp.dot(a.astype(BF16), v_ref[0, keys, _head(g)],
                                  preferred_element_type=F32)
            r_ref[g] = r_prev + incls[g][:, 0:1]

    r_ref[...] = jnp.zeros_like(r_ref)
    acc_ref[...] = jnp.zeros_like(acc_ref)
    tile(i, True)

    zero_bits = -EXP_ZERO * LOG2E

    def cond(carry):
        j, live = carry
        return jnp.logical_and(j >= 0, live)

    def body(carry):
        j, _ = carry
        tile(j, False)
        return j - 1, jnp.min(r_ref[...]) <= zero_bits

    lax.while_loop(cond, body, (i - 1, jnp.min(r_ref[...]) <= zero_bits))
    for g in range(heads):
        o_ref[0, :, _head(g)] = acc_ref[g].astype(o_ref.dtype)


def _fox_block_counts(c, order, score_bound, t, heads):
    b, s, _ = c.shape
    n_blk = s // t
    ch = c[:, :, :N_HEADS][:, :, order]
    c_first_q = ch[:, ::t, :]
    c_last_k = ch[:, t - 1::t, :]
    decay = c_first_q[:, :, None, :] - c_last_k[:, None, :, :]
    live = decay >= (EXP_ZERO - 1.0) - 2.0 * score_bound
    qi = jnp.arange(n_blk)[None, :, None, None]
    kj = jnp.arange(n_blk)[None, None, :, None]
    live = jnp.logical_and(live, kj < qi)
    n = jnp.sum(live, axis=2).astype(jnp.int32)
    n = n.reshape(b, n_blk, N_HEADS // heads, heads).max(axis=-1)
    return jnp.transpose(n, (0, 2, 1)).reshape(-1)


def _fox_attention(q, k, v, c, aug_q, aug_k, order, score_bound, *, t=512, heads=4):
    b, s, d = q.shape
    counts = _fox_block_counts(c, order, score_bound, t, heads)

    def q_spec(g):
        return pl.BlockSpec((1, t, LANES),
                            lambda bi, h, i, order, nb: (bi, i, order[h * heads + g]))

    def kv_spec(g):
        return pl.BlockSpec((1, s, LANES),
                            lambda bi, h, i, order, nb: (bi, 0, order[h * heads + g]))

    q_specs = [q_spec(g) for g in range(heads)]
    kv_specs = [kv_spec(g) for g in range(heads)]
    return pl.pallas_call(
        functools.partial(_fox_attn_kernel, heads=heads),
        grid_spec=pltpu.PrefetchScalarGridSpec(
            num_scalar_prefetch=2,
            grid=(b, N_HEADS // heads, s // t),
            in_specs=q_specs + q_specs + kv_specs + kv_specs + kv_specs,
            out_specs=pl.BlockSpec((1, t, heads * LANES),
                                   lambda bi, h, i, order, nb: (bi, i, h)),
            scratch_shapes=[pltpu.VMEM((heads, t, LANES), F32),
                            pltpu.VMEM((heads, t, 2 * LANES), F32),
                            pltpu.VMEM((heads, t, t), F32)],
        ),
        out_shape=jax.ShapeDtypeStruct((b, s, d), BF16),
        compiler_params=_params(("parallel", "parallel", "arbitrary")),
        name="fox_attn",
    )(order, counts,
      *([q] * heads + [aug_q] * heads + [k] * heads + [aug_k] * heads + [v] * heads))


def _sb_attention(q, k, v, *, t=256, heads=8):
    b, s, d = q.shape
    w = heads * LANES
    qo_spec = pl.BlockSpec((1, t, w), lambda bi, h, i: (bi, i, h))
    kv_spec = pl.BlockSpec((1, s, w), lambda bi, h, i: (bi, 0, h))
    return pl.pallas_call(
        functools.partial(_sb_attn_kernel, heads=heads),
        grid=(b, N_HEADS // heads, s // t),
        in_specs=[qo_spec, kv_spec, kv_spec],
        out_specs=qo_spec,
        out_shape=jax.ShapeDtypeStruct((b, s, d), BF16),
        scratch_shapes=[pltpu.VMEM((heads, t, LANES), F32)] * 2,
        compiler_params=_params(("parallel", "parallel", "arbitrary")),
        name="sb_attn",
    )(q, k, v)


def _oproj_kernel(m_ref, w_ref, x_ref, o_ref):
    o_ref[...] = x_ref[...] + jnp.dot(m_ref[...], w_ref[...], preferred_element_type=F32)


def _oproj(m2d, w, x2d, *, tm=512):
    t, d = x2d.shape
    row_spec = pl.BlockSpec((tm, d), lambda i: (i, 0))
    return pl.pallas_call(
        _oproj_kernel,
        grid=(t // tm,),
        in_specs=[row_spec, pl.BlockSpec((d, d), lambda i: (0, 0)), row_spec],
        out_specs=row_spec,
        out_shape=jax.ShapeDtypeStruct((t, d), F32),
        compiler_params=_params(("parallel",)),
        name="oproj",
    )(m2d, w, x2d)


def kernel(x, norm_g, ffn_w_in, ffn_w_out, fox_w_in, fox_b_f, fox_qk_g, sb_w_in, w_o):
    b, s, d = x.shape
    depth = norm_g.shape[0]
    head_dim = d // N_HEADS
    t = b * s
    x2d = x.reshape(t, d)
    ffn_w_in, ffn_w_out = ffn_w_in.astype(BF16), ffn_w_out.astype(BF16)
    fox_w, sb_w, w_o = fox_w_in.astype(BF16), sb_w_in.astype(BF16), w_o.astype(BF16)
    for i in range(depth):
        g = norm_g[i]
        x2d = _ffn(x2d, g[0], ffn_w_in, ffn_w_out, i, 0)
        j = i // N_MIXERS
        if i % N_MIXERS == 0:
            w_f = jnp.pad(fox_w[j, :, 3 * d:], ((0, 0), (0, LANES - N_HEADS)))
            b_f = jnp.pad(fox_b_f[j], (0, LANES - N_HEADS)).reshape(1, LANES)
            g_q, g_k = fox_qk_g[j, 0:1], fox_qk_g[j, 1:2]
            q, k, v, lf = _proj(x2d, g[1], fox_w, j, (w_f, b_f, g_q, g_k))
            c, aug_q, aug_k = _cumsum_seq(lf.reshape(b, s, LANES), d)
            score_bound = (head_dim ** 0.5 * (1.0 + BF16_ULP) ** 2
                           * jnp.max(jnp.abs(g_q)) * jnp.max(jnp.abs(g_k)))
            order = jnp.argsort(jnp.sum(c[:, s - 1, :N_HEADS], axis=0)).astype(jnp.int32)
            m = _fox_attention(q.reshape(b, s, d), k.reshape(b, s, d), v.reshape(b, s, d),
                               c, aug_q, aug_k, order, score_bound)
            w_out_proj = w_o[i].reshape(N_HEADS, head_dim, d)[order].reshape(d, d)
        else:
            q, k, v = _proj(x2d, g[1], sb_w, j)
            m = _sb_attention(q.reshape(b, s, d), k.reshape(b, s, d), v.reshape(b, s, d))
            w_out_proj = w_o[i]
        x2d = _oproj(m.reshape(t, d), w_out_proj, x2d)
        x2d = _ffn(x2d, g[2], ffn_w_in, ffn_w_out, i, 1)
    return x2d.reshape(b, s, d)
```
